```python
import functools
import jax, jax.numpy as jnp
from jax import lax
import numpy as np

D_MODEL = 1024
BATCH = 4
SEQ = 4096
DEPTH = 4
DEC_BATCH = 128
DEC_SEQ = 8
PAST_LEN = 8192
PAGE_SIZE = 128

CONV_DIM = D_MODEL // 4
CONV_GROUPS = 4
CONV_W = 3
MLA_HEADS = 8
MLA_NOPE = D_MODEL // 16
MLA_ROPE = MLA_NOPE // 2
MLA_V = D_MODEL // 16
MLA_Q_LORA = D_MODEL // 4
MLA_KV_LORA = D_MODEL // 8
GLA_HEADS = 4
GLA_DK = D_MODEL // 32
GLA_DV = D_MODEL // 16
GLA_GATE_RANK = 16
GLA_NORMALIZER = 16.0
GLA_CHUNK = 64
N_EXPERTS = 32
TOP_K = 4
D_FF = D_MODEL
SWIGLU_ALPHA = 1.702
SWIGLU_LIMIT = 7.0
MOE_BLOCK = 128
ROPE_THETA = 10000.0
EPS = 1e-6
Q_BLOCK = 128
N_MOD = 6
IN_SIZES = (CONV_DIM, CONV_DIM, CONV_DIM,
            MLA_Q_LORA, MLA_KV_LORA, MLA_ROPE,
            GLA_HEADS * GLA_DK, GLA_HEADS * GLA_DK, GLA_HEADS * GLA_DV, GLA_GATE_RANK, GLA_HEADS * GLA_DV)
D_IN = sum(IN_SIZES)

kernel_name = 'hybrid_conv_mla_gla_moe_adaln_step'


def rmsnorm(x, g):
    xf = x.astype(jnp.float32)
    y = xf * lax.rsqrt(jnp.mean(xf * xf, axis=-1, keepdims=True) + EPS)
    return (y * g.astype(jnp.float32)).astype(x.dtype)


def rope(x, pos):
    half = MLA_ROPE // 2
    freqs = ROPE_THETA ** (-jnp.arange(half, dtype=jnp.float32) / half)
    ang = pos.astype(jnp.float32)[:, None] * freqs
    ang = ang.reshape((ang.shape[0],) + (1,) * (x.ndim - 3) + (half,))
    cos, sin = jnp.cos(ang), jnp.sin(ang)
    xf = x.astype(jnp.float32)
    x1, x2 = xf[..., :half], xf[..., half:]
    return jnp.concatenate([x1 * cos - x2 * sin, x2 * cos + x1 * sin], axis=-1).astype(x.dtype)


def short_conv(u, prev, w):
    t = u.shape[1]
    up = jnp.concatenate([prev.astype(u.dtype), u], axis=1)
    y = up[:, 0:t] * w[0]
    for j in range(1, CONV_W):
        y = y + up[:, j:j + t] * w[j]
    return y, up[:, t:]


def gla_chunked(q, k, v, log_a, s0):
    b, t, nh, _ = q.shape
    dv = v.shape[-1]
    L = min(GLA_CHUNK, t)
    n = -(-t // L)
    pad = n * L - t

    def chunks(a):
        a = jnp.pad(a.astype(jnp.float32), ((0, 0), (0, pad), (0, 0), (0, 0)))
        return a.reshape(b, n, L, nh, a.shape[-1]).transpose(1, 0, 3, 2, 4)

    mask = jnp.tril(jnp.ones((L, L), dtype=bool))[:, :, None]

    def step(S, inp):
        qc, kc, vc, ac = inp
        cum = jnp.cumsum(ac, axis=2)
        o_inter = jnp.einsum('bhtk,bhkv->bhtv', qc * jnp.exp(cum), S)
        diff = cum[:, :, :, None, :] - cum[:, :, None, :, :]
        dec = jnp.exp(jnp.where(mask, diff, -jnp.inf))
        att = jnp.einsum('bhtk,bhsk,bhtsk->bhts', qc, kc, dec)
        o_intra = jnp.einsum('bhts,bhsv->bhtv', att, vc)
        last = cum[:, :, -1:, :]
        S = jnp.exp(last[:, :, 0, :])[..., None] * S + jnp.einsum('bhsk,bhsv->bhkv', kc * jnp.exp(last - cum), vc)
        return S, o_inter + o_intra

    S, o = lax.scan(step, s0.astype(jnp.float32), (chunks(q), chunks(k), chunks(v), chunks(log_a)))
    o = o.transpose(1, 0, 3, 2, 4).reshape(b, n * L, nh, dv)[:, :t]
    return o.astype(q.dtype), S.astype(s0.dtype)


def mla_core(qn, qr, q_pos, ckv, kn, kr, k_pos):
    s = (jnp.einsum('thd,shd->hts', qn, kn, preferred_element_type=jnp.float32)
         + jnp.einsum('thd,sd->hts', qr, kr, preferred_element_type=jnp.float32))
    s = s * ((MLA_NOPE + MLA_ROPE) ** -0.5)
    causal = k_pos[None, :] <= q_pos[:, None]
    s = jnp.where(causal[None], s, -jnp.inf)
    p = jax.nn.softmax(s, axis=-1)
    return jnp.einsum('hts,sr->thr', p.astype(ckv.dtype), ckv)


def decompress_keys(ckv, w_uk, g_kn):
    return rmsnorm(jnp.einsum('...sr,rhd->...shd', ckv, w_uk), g_kn)


def mla_attn_prompt(qn, qr, ckv, kr, pos, w_uk, g_kn):
    b, t, nh, _ = qn.shape
    kn = decompress_keys(ckv, w_uk, g_kn)
    qb = Q_BLOCK if t % Q_BLOCK == 0 else t
    nb = t // qb
    qn_b = qn.reshape(b, nb, qb, nh, MLA_NOPE).swapaxes(0, 1)
    qr_b = qr.reshape(b, nb, qb, nh, MLA_ROPE).swapaxes(0, 1)
    pos_b = pos.reshape(nb, qb)
    core_b = jax.vmap(mla_core, in_axes=(0, 0, None, 0, 0, 0, None))

    def block(args):
        qn_i, qr_i, pos_i = args
        return core_b(qn_i, qr_i, pos_i, ckv, kn, kr, pos)

    o = lax.map(block, (qn_b, qr_b, pos_b))
    return o.swapaxes(0, 1).reshape(b, t, nh, MLA_KV_LORA)


def mla_attn_sample(qn, qr, ckv, kr, pos, cache_ckv, cache_kr, page_table, layer, w_uk, g_kn):
    past = page_table.shape[1] * PAGE_SIZE
    k_pos = jnp.concatenate([jnp.arange(past, dtype=jnp.int32), pos])

    def one(args):
        qn_b, qr_b, ckv_b, kr_b, pages = args
        ckv_all = jnp.concatenate([cache_ckv[layer, pages].reshape(past, MLA_KV_LORA), ckv_b.astype(cache_ckv.dtype)], axis=0)
        kr_all = jnp.concatenate([cache_kr[layer, pages].reshape(past, MLA_ROPE), kr_b.astype(cache_kr.dtype)], axis=0)
        kn = decompress_keys(ckv_all, w_uk, g_kn)
        return mla_core(qn_b, qr_b, pos, ckv_all, kn, kr_all, k_pos)

    return lax.map(one, (qn, qr, ckv, kr, page_table))


def moe(h, lp):
    n, d = h.shape
    logits = jnp.einsum('nd,de->ne', h, lp['w_router'], preferred_element_type=jnp.float32) + lp['b_router'].astype(jnp.float32)
    top_v, top_i = lax.top_k(logits, TOP_K)
    gates = jax.nn.softmax(top_v, axis=-1)
    m = n * TOP_K
    flat_e = top_i.reshape(m)
    flat_t = jnp.repeat(jnp.arange(n, dtype=jnp.int32), TOP_K)
    flat_g = gates.reshape(m)
    order = jnp.argsort(flat_e)
    se, st, sg = flat_e[order], flat_t[order], flat_g[order]
    counts = jnp.zeros((N_EXPERTS,), jnp.int32).at[flat_e].add(1)
    padded = (counts + MOE_BLOCK - 1) // MOE_BLOCK * MOE_BLOCK
    starts = jnp.cumsum(counts) - counts
    pends = jnp.cumsum(padded)
    pstarts = pends - padded
    dest = pstarts[se] + jnp.arange(m, dtype=jnp.int32) - starts[se]
    p_rows = -(-(m + N_EXPERTS * (MOE_BLOCK - 1)) // MOE_BLOCK) * MOE_BLOCK
    nblk = p_rows // MOE_BLOCK
    buf_t = jnp.full((p_rows,), n, jnp.int32).at[dest].set(st)
    buf_g = jnp.zeros((p_rows,), jnp.float32).at[dest].set(sg)
    blk_start = jnp.arange(nblk, dtype=jnp.int32) * MOE_BLOCK
    blk_e = jnp.minimum(jnp.sum(blk_start[:, None] >= pends[None, :], axis=1), N_EXPERTS - 1)
    h_pad = jnp.concatenate([h, jnp.zeros((1, d), h.dtype)], axis=0)
    xb = h_pad[buf_t].reshape(nblk, MOE_BLOCK, d)

    def expert_block(args):
        xe, e = args
        g = xe @ lp['w_gate'][e] + lp['b_gate'][e]
        u = xe @ lp['w_up'][e] + lp['b_up'][e]
        g = jnp.minimum(g, SWIGLU_LIMIT)
        u = jnp.clip(u, -SWIGLU_LIMIT, SWIGLU_LIMIT)
        a = (u + 1) * (g * jax.nn.sigmoid(SWIGLU_ALPHA * g))
        return a @ lp['w_down'][e] + lp['b_down'][e]

    yb = lax.map(expert_block, (xb, blk_e)).reshape(p_rows, d)
    y = jnp.zeros((n + 1, d), jnp.float32).at[buf_t].add(yb.astype(jnp.float32) * buf_g[:, None])
    return y[:n].astype(h.dtype)


def parallel_mixer(h, pos, conv_prev, gla_prev, attend, lp):
    b, t, _ = h.shape
    offs = np.cumsum(IN_SIZES)[:-1].tolist()
    (u_h, u_c, u_b, q_lat, kv_lat, k_rope, gq, gk, gv, ga, gr) = jnp.split(h @ lp['w_in'], offs, axis=-1)
    conv_y, conv_state = short_conv(u_c * u_h, conv_prev, lp['w_conv'])
    y_a = u_b * conv_y
    q = jnp.einsum('btr,rhd->bthd', rmsnorm(q_lat, lp['g_qa']), lp['w_uq'])
    qn = rmsnorm(q[..., :MLA_NOPE], lp['g_qn'])
    qr = rope(rmsnorm(q[..., MLA_NOPE:], lp['g_qr']), pos)
    ckv = rmsnorm(kv_lat, lp['g_kva'])
    kr = rope(rmsnorm(k_rope, lp['g_kr']), pos)
    o_lat = attend(qn, qr, ckv, kr)
    y_b = jnp.einsum('bthr,rhv->bthv', o_lat, lp['w_uv']).reshape(b, t, MLA_HEADS * MLA_V)
    gq = gq.reshape(b, t, GLA_HEADS, GLA_DK) * (GLA_DK ** -0.5)
    gk = gk.reshape(b, t, GLA_HEADS, GLA_DK)
    gv = gv.reshape(b, t, GLA_HEADS, GLA_DV)
    log_a = jax.nn.log_sigmoid((ga @ lp['w_a2'] + lp['b_a']).astype(jnp.float32)) / GLA_NORMALIZER
    log_a = log_a.reshape(b, t, GLA_HEADS, GLA_DK)
    o, gla_state = gla_chunked(gq, gk, gv, log_a, gla_prev)
    y_c = (rmsnorm(o, lp['g_gla_o']) * jax.nn.silu(gr.reshape(b, t, GLA_HEADS, GLA_DV))).reshape(b, t, GLA_HEADS * GLA_DV)
    y = jnp.concatenate([y_a, y_b, y_c], axis=-1) @ lp['w_o']
    return y, conv_state, gla_state, ckv, kr


def decoder_layer(x, c, pos, conv_prev, gla_prev, attend, lp):
    b, _, d = x.shape
    mod = (jax.nn.silu(c) @ lp['w_mod'] + lp['b_mod']).reshape(b, N_MOD, 1, d)
    sh_m, sc_m, gt_m, sh_f, sc_f, gt_f = (mod[:, i] for i in range(N_MOD))
    h = rmsnorm(x, lp['g_mix']) * (1 + sc_m) + sh_m
    y, conv_s, gla_s, ckv, kr = parallel_mixer(h, pos, conv_prev, gla_prev, attend, lp)
    x = x + gt_m * y
    h = rmsnorm(x, lp['g_ffn']) * (1 + sc_f) + sh_f
    x = x + gt_f * moe(h.reshape(-1, d), lp).reshape(x.shape)
    return x, conv_s, gla_s, ckv, kr


def setup_inputs(seed: int = 0) -> dict:
    key = jax.random.key(seed)
    ks = iter(jax.random.split(key, 48))
    n_pages = PAST_LEN // PAGE_SIZE
    used = DEC_BATCH * n_pages
    n_pool = used + max(1, used // 4)
    f32 = jnp.float32

    def nrm(shape, scale):
        return jax.random.normal(next(ks), shape, f32) * scale

    def gain(shape):
        return 1.0 + nrm(shape, 0.05)

    inp = {}
    inp['x_prompt'] = nrm((BATCH, SEQ, D_MODEL), 1.0)
    inp['x_sample'] = nrm((DEC_BATCH, DEC_SEQ, D_MODEL), 1.0)
    inp['cache_ckv'] = nrm((DEPTH, n_pool, PAGE_SIZE, MLA_KV_LORA), 1.0)
    inp['cache_kr'] = nrm((DEPTH, n_pool, PAGE_SIZE, MLA_ROPE), 1.0)
    inp['state_conv'] = nrm((DEPTH, DEC_BATCH, CONV_W - 1, CONV_DIM), 1.0)
    inp['state_gla'] = nrm((DEPTH, DEC_BATCH, GLA_HEADS, GLA_DK, GLA_DV), 1.0)
    inp['page_table'] = jax.random.permutation(next(ks), n_pool)[:used].reshape(DEC_BATCH, n_pages).astype(jnp.int32)
    inp['c_prompt'] = nrm((BATCH, D_MODEL), 1.0)
    inp['c_sample'] = nrm((DEC_BATCH, D_MODEL), 1.0)
    inp['g_mix'] = gain((DEPTH, D_MODEL))
    inp['g_ffn'] = gain((DEPTH, D_MODEL))
    inp['w_mod'] = nrm((DEPTH, D_MODEL, N_MOD * D_MODEL), 0.5 * D_MODEL ** -0.5)
    inp['b_mod'] = nrm((DEPTH, N_MOD * D_MODEL), 0.02)
    inp['w_in'] = nrm((DEPTH, D_MODEL, D_IN), D_MODEL ** -0.5)
    inp['w_o'] = nrm((DEPTH, D_MODEL, D_MODEL), D_MODEL ** -0.5)
    inp['w_conv'] = nrm((DEPTH, CONV_W, CONV_DIM), CONV_W ** -0.5)
    inp['g_qa'] = gain((DEPTH, MLA_Q_LORA))
    inp['w_uq'] = nrm((DEPTH, MLA_Q_LORA, MLA_HEADS, MLA_NOPE + MLA_ROPE), MLA_Q_LORA ** -0.5)
    inp['g_qn'] = gain((DEPTH, MLA_NOPE))
    inp['g_qr'] = gain((DEPTH, MLA_ROPE))
    inp['g_kva'] = gain((DEPTH, MLA_KV_LORA))
    inp['g_kr'] = gain((DEPTH, MLA_ROPE))
    inp['w_uk'] = nrm((DEPTH, MLA_KV_LORA, MLA_HEADS, MLA_NOPE), MLA_KV_LORA ** -0.5)
    inp['g_kn'] = gain((DEPTH, MLA_NOPE))
    inp['w_uv'] = nrm((DEPTH, MLA_KV_LORA, MLA_HEADS, MLA_V), MLA_KV_LORA ** -0.5)
    inp['w_a2'] = nrm((DEPTH, GLA_GATE_RANK, GLA_HEADS * GLA_DK), GLA_GATE_RANK ** -0.5)
    inp['b_a'] = nrm((DEPTH, GLA_HEADS * GLA_DK), 0.1)
    inp['g_gla_o'] = gain((DEPTH, GLA_DV))
    inp['w_router'] = nrm((DEPTH, D_MODEL, N_EXPERTS), D_MODEL ** -0.5)
    inp['b_router'] = nrm((DEPTH, N_EXPERTS), 0.01)
    inp['w_gate'] = nrm((DEPTH, N_EXPERTS, D_MODEL, D_FF), D_MODEL ** -0.5)
    inp['b_gate'] = nrm((DEPTH, N_EXPERTS, D_FF), 0.02)
    inp['w_up'] = nrm((DEPTH, N_EXPERTS, D_MODEL, D_FF), D_MODEL ** -0.5)
    inp['b_up'] = nrm((DEPTH, N_EXPERTS, D_FF), 0.02)
    inp['w_down'] = nrm((DEPTH, N_EXPERTS, D_FF, D_MODEL), D_FF ** -0.5)
    inp['b_down'] = nrm((DEPTH, N_EXPERTS, D_MODEL), 0.02)
    return inp


def reference(x_prompt, x_sample, cache_ckv, cache_kr, state_conv, state_gla, page_table, c_prompt, c_sample,
              g_mix, g_ffn, w_mod, b_mod, w_in, w_o, w_conv, g_qa, w_uq, g_qn, g_qr, g_kva, g_kr, w_uk, g_kn, w_uv,
              w_a2, b_a, g_gla_o, w_router, b_router, w_gate, b_gate, w_up, b_up, w_down, b_down):
    bp, tp, _ = x_prompt.shape
    ts = x_sample.shape[1]
    past = page_table.shape[1] * PAGE_SIZE
    pos_p = jnp.arange(tp, dtype=jnp.int32)
    pos_s = past + jnp.arange(ts, dtype=jnp.int32)
    conv0 = jnp.zeros((bp, CONV_W - 1, CONV_DIM), x_prompt.dtype)
    gla0 = jnp.zeros((bp, GLA_HEADS, GLA_DK, GLA_DV), x_prompt.dtype)
    xp, xs = x_prompt, x_sample
    ckv_p, kr_p, conv_p, gla_p = [], [], [], []
    ckv_s, kr_s, conv_s, gla_s = [], [], [], []
    for l in range(DEPTH):
        lp = dict(g_mix=g_mix[l], g_ffn=g_ffn[l], w_mod=w_mod[l], b_mod=b_mod[l], w_in=w_in[l], w_o=w_o[l],
                  w_conv=w_conv[l], g_qa=g_qa[l], w_uq=w_uq[l], g_qn=g_qn[l], g_qr=g_qr[l], g_kva=g_kva[l],
                  g_kr=g_kr[l], w_uv=w_uv[l], w_a2=w_a2[l], b_a=b_a[l], g_gla_o=g_gla_o[l],
                  w_router=w_router[l], b_router=b_router[l], w_gate=w_gate[l], b_gate=b_gate[l],
                  w_up=w_up[l], b_up=b_up[l], w_down=w_down[l], b_down=b_down[l])
        attend_p = functools.partial(mla_attn_prompt, pos=pos_p, w_uk=w_uk[l], g_kn=g_kn[l])
        attend_s = functools.partial(mla_attn_sample, pos=pos_s, cache_ckv=cache_ckv, cache_kr=cache_kr,
                                     page_table=page_table, layer=l, w_uk=w_uk[l], g_kn=g_kn[l])
        xp, cv, gs, ck, kr = decoder_layer(xp, c_prompt, pos_p, conv0, gla0, attend_p, lp)
        ckv_p.append(ck); kr_p.append(kr); conv_p.append(cv); gla_p.append(gs)
        xs, cv, gs, ck, kr = decoder_layer(xs, c_sample, pos_s, state_conv[l], state_gla[l], attend_s, lp)
        ckv_s.append(ck); kr_s.append(kr); conv_s.append(cv); gla_s.append(gs)
    return (xp, xs,
            jnp.stack(ckv_p), jnp.stack(kr_p), jnp.stack(conv_p), jnp.stack(gla_p),
            jnp.stack(ckv_s), jnp.stack(kr_s), jnp.stack(conv_s), jnp.stack(gla_s))
```

```python
import functools

import numpy as np
import jax
import jax.numpy as jnp
from jax import lax
from jax.experimental import pallas as pl
from jax.experimental.pallas import tpu as pltpu

F32 = jnp.float32
BF16 = jnp.bfloat16

D_MODEL = 1024
CONV_DIM = 256
CONV_W = 3
MLA_HEADS = 8
MLA_NOPE = 64
MLA_ROPE = 32
MLA_V = 64
MLA_Q_LORA = 256
MLA_KV_LORA = 128
GLA_HEADS = 4
GLA_DK = 32
GLA_DV = 64
GLA_GATE_RANK = 16
GLA_NORMALIZER = 16.0
N_EXPERTS = 32
TOP_K = 4
D_FF = 1024
SWIGLU_ALPHA = 1.702
SWIGLU_LIMIT = 7.0
ROPE_THETA = 10000.0
EPS = 1e-6
N_MOD = 6
PAGE_SIZE = 128
IN_SIZES = (256, 256, 256, 256, 128, 32, 128, 128, 256, 16, 256)

LANES = 128
SUBLANES = 8

W_CONV_OUT = 3 * CONV_DIM
W_MLA_OUT = MLA_Q_LORA + MLA_KV_LORA + LANES
W_GLA_OUT = 128 + 128 + 256 + LANES + 256
W_IN_OUT = W_CONV_OUT + W_MLA_OUT + W_GLA_OUT
HEAD_W = LANES
QK_SCALE = (MLA_NOPE + MLA_ROPE) ** -0.5
NEG_BIG = -1e30

MOE_TM = 256
DISPATCH_ROWS = 2048
COMBINE_TOK = 128


def _dot(a, b):
    return jnp.dot(a, b, preferred_element_type=F32)


def _dot_nt(a, b):
    return lax.dot_general(a, b, (((1,), (1,)), ((), ())), preferred_element_type=F32)


def _dot_tn(a, b):
    return lax.dot_general(a, b, (((0,), (0,)), ((), ())), preferred_element_type=F32)


def _sigmoid(x):
    return 1.0 / (1.0 + jnp.exp(-x))


def _params(sem, vmem_mb):
    return pltpu.CompilerParams(dimension_semantics=sem, vmem_limit_bytes=vmem_mb * 1024 * 1024)


def _mod_kernel(c_ref, w_ref, b_ref, o_ref):
    c = c_ref[...]
    a = (c * _sigmoid(c)).astype(BF16)
    o_ref[...] = _dot(a, w_ref[...].astype(BF16)) + b_ref[...]


def _mod_call(c_all, w_mod, b_mod):
    depth = w_mod.shape[0]
    nb = c_all.shape[0]
    return pl.pallas_call(
        _mod_kernel,
        grid=(depth, N_MOD),
        in_specs=[
            pl.BlockSpec((nb, D_MODEL), lambda l, j: (0, 0)),
            pl.BlockSpec((None, D_MODEL, D_MODEL), lambda l, j: (l, 0, j)),
            pl.BlockSpec((None, 1, D_MODEL), lambda l, j: (l, 0, j)),
        ],
        out_specs=pl.BlockSpec((None, nb, D_MODEL), lambda l, j: (l, 0, j)),
        out_shape=jax.ShapeDtypeStruct((depth, nb, N_MOD * D_MODEL), F32),
        compiler_params=_params(("arbitrary", "arbitrary"), 40),
        name="adaln_mod",
    )(c_all, w_mod, b_mod.reshape(depth, 1, N_MOD * D_MODEL))


def _in_kernel(x_ref, mod_ref, g_ref, w_ref, oc_ref, om_ref, og_ref):
    bb, tt, _ = x_ref.shape
    x = x_ref[...]
    ms = jnp.mean(x * x, axis=-1, keepdims=True)
    y = x * lax.rsqrt(ms + EPS) * g_ref[...]
    h = y * (1.0 + mod_ref[:, 1:2, :]) + mod_ref[:, 0:1, :]
    h2 = h.reshape(bb * tt, D_MODEL).astype(BF16)
    o = _dot(h2, w_ref[...])
    oc_ref[...] = o[:, 0:W_CONV_OUT].reshape(bb, tt, W_CONV_OUT)
    om_ref[...] = o[:, W_CONV_OUT:W_CONV_OUT + W_MLA_OUT].reshape(bb, tt, W_MLA_OUT)
    og_ref[...] = o[:, W_CONV_OUT + W_MLA_OUT:].reshape(bb, tt, W_GLA_OUT)


def _in_call(x, mod, g_mix, w_in_p, layer, bb, tt):
    b, t, _ = x.shape
    bs3 = lambda w: pl.BlockSpec((bb, tt, w), lambda i, j: (i, j, 0))
    return pl.pallas_call(
        _in_kernel,
        grid=(b // bb, t // tt),
        in_specs=[
            bs3(D_MODEL),
            pl.BlockSpec((bb, N_MOD, D_MODEL), lambda i, j: (i, 0, 0)),
            pl.BlockSpec((None, 1, D_MODEL), lambda i, j: (layer, 0, 0)),
            pl.BlockSpec((None, D_MODEL, W_IN_OUT), lambda i, j: (layer, 0, 0)),
        ],
        out_specs=[bs3(W_CONV_OUT), bs3(W_MLA_OUT), bs3(W_GLA_OUT)],
        out_shape=[jax.ShapeDtypeStruct((b, t, W_CONV_OUT), F32),
                   jax.ShapeDtypeStruct((b, t, W_MLA_OUT), F32),
                   jax.ShapeDtypeStruct((b, t, W_GLA_OUT), F32)],
        compiler_params=_params(("arbitrary", "arbitrary"), 48),
        name="in_proj",
    )(x, mod, g_mix, w_in_p)


def _segsum(x2, p_ref):
    return _dot(x2.astype(BF16), p_ref[...])


def _swap_rope_halves(x):
    lane = lax.broadcasted_iota(jnp.int32, x.shape, 1)
    return jnp.where(lane < 80, pltpu.roll(x, LANES - 16, 1), pltpu.roll(x, 16, 1))


def _qk_kernel(mla_ref, cos_ref, sin_ref, gqa_ref, wuq_ref, gq_ref, gkva_ref, wuk_ref, gk_ref, gkr_ref,
               p_ref, invn_ref, q_out, k_out, ckv_out, kr_out):
    bb, tt, _ = mla_ref.shape
    rows = bb * tt
    mla = mla_ref[...].reshape(rows, W_MLA_OUT)
    cos = cos_ref[...]
    sin = sin_ref[...]
    invn = invn_ref[...]

    def head_norm_rope(xh, gain):
        rstd = lax.rsqrt(_segsum(xh * xh, p_ref) * invn + EPS)
        xn = xh * rstd * gain
        return xn * cos + _swap_rope_halves(xn) * sin

    q_lat = mla[:, 0:MLA_Q_LORA]
    qa = q_lat * lax.rsqrt(jnp.mean(q_lat * q_lat, axis=-1, keepdims=True) + EPS) * gqa_ref[...]
    q = _dot(qa.astype(BF16), wuq_ref[...])
    kv_lat = mla[:, MLA_Q_LORA:MLA_Q_LORA + MLA_KV_LORA]
    ckv = kv_lat * lax.rsqrt(jnp.mean(kv_lat * kv_lat, axis=-1, keepdims=True) + EPS) * gkva_ref[...]
    ckv_out[...] = ckv.reshape(bb, tt, MLA_KV_LORA)
    kr = head_norm_rope(mla[:, MLA_Q_LORA + MLA_KV_LORA:], gkr_ref[...])
    kr_out[...] = kr.reshape(bb, tt, LANES)
    k = _dot(ckv.astype(BF16), wuk_ref[...])
    for h in range(MLA_HEADS):
        qh = head_norm_rope(q[:, h * HEAD_W:(h + 1) * HEAD_W], gq_ref[...]) * QK_SCALE
        q_out[:, h, :, :] = qh.astype(q_out.dtype).reshape(bb, tt, HEAD_W)
        kh = k[:, h * HEAD_W:(h + 1) * HEAD_W]
        rstd = lax.rsqrt(_segsum(kh * kh, p_ref) * invn + EPS)
        k_out[:, h, :, :] = (kh * rstd * gk_ref[...] + kr).astype(k_out.dtype).reshape(bb, tt, HEAD_W)


def _qk_call(mla, cos_t, sin_t, tab_index, wp, layer, bb, tt, qk_dtype):
    b, t, _ = mla.shape
    rows = bb * tt
    vec = lambda arr: pl.BlockSpec((None, 1, arr.shape[-1]), lambda i, j: (layer, 0, 0))
    mat = lambda arr: pl.BlockSpec((None,) + arr.shape[1:], lambda i, j: (layer, 0, 0))
    tab = pl.BlockSpec((rows, LANES), lambda i, j: (tab_index(i, j), 0))
    const2 = lambda arr: pl.BlockSpec(arr.shape, lambda i, j: (0, 0))
    hk = pl.BlockSpec((bb, MLA_HEADS, tt, HEAD_W), lambda i, j: (i, 0, j, 0))
    return pl.pallas_call(
        _qk_kernel,
        grid=(b // bb, t // tt),
        in_specs=[pl.BlockSpec((bb, tt, W_MLA_OUT), lambda i, j: (i, j, 0)), tab, tab,
                  vec(wp["g_qa"]), mat(wp["w_uq"]), vec(wp["gq_lane"]), vec(wp["g_kva"]), mat(wp["w_uk"]),
                  vec(wp["gk_lane"]), vec(wp["gkr_lane"]), const2(wp["p_seg"]), const2(wp["invn"])],
        out_specs=[hk, hk,
                   pl.BlockSpec((bb, tt, MLA_KV_LORA), lambda i, j: (i, j, 0)),
                   pl.BlockSpec((bb, tt, LANES), lambda i, j: (i, j, 0))],
        out_shape=[jax.ShapeDtypeStruct((b, MLA_HEADS, t, HEAD_W), qk_dtype),
                   jax.ShapeDtypeStruct((b, MLA_HEADS, t, HEAD_W), qk_dtype),
                   jax.ShapeDtypeStruct((b, t, MLA_KV_LORA), F32),
                   jax.ShapeDtypeStruct((b, t, LANES), F32)],
        compiler_params=_params(("arbitrary", "arbitrary"), 40),
        name="mla_prep",
    )(mla, cos_t, sin_t, wp["g_qa"], wp["w_uq"], wp["gq_lane"], wp["g_kva"], wp["w_uk"], wp["gk_lane"],
      wp["gkr_lane"], wp["p_seg"], wp["invn"])


def _attn_prompt_kernel(q_ref, k_ref, v_ref, wuv_ref, o_ref, *, tq):
    qi = pl.program_id(1)
    row = lax.broadcasted_iota(jnp.int32, (tq, tq), 0)
    col = lax.broadcasted_iota(jnp.int32, (tq, tq), 1)
    causal = col <= row
    y = jnp.zeros((tq, MLA_HEADS * MLA_V), F32)
    for h in range(MLA_HEADS):
        q = q_ref[h]

        def step(kb, carry, masked):
            m, l, acc = carry
            off = pl.multiple_of(kb * tq, tq)
            k = k_ref[h, pl.ds(off, tq), :]
            v = v_ref[pl.ds(off, tq), :]
            s = _dot_nt(q, k)
            if masked:
                s = jnp.where(causal, s, -jnp.inf)
            m_new = jnp.maximum(m, jnp.max(s, axis=-1, keepdims=True))
            p = jnp.exp(s - m_new)
            alpha = jnp.exp(m - m_new)
            l = alpha * l + jnp.sum(p, axis=-1, keepdims=True)
            acc = alpha * acc + _dot(p.astype(BF16), v)
            return m_new, l, acc

        init = (jnp.full((tq, 1), -jnp.inf, F32), jnp.zeros((tq, 1), F32), jnp.zeros((tq, MLA_KV_LORA), F32))
        carry = lax.fori_loop(0, qi, lambda kb, c: step(kb, c, False), init)
        _, l, acc = step(qi, carry, True)
        o = acc / l
        y = y + _dot(o.astype(BF16), wuv_ref[h])
    o_ref[...] = y


def _attn_prompt_call(q_cat, k_cat, ckv_bf, w_uvp, layer, tq):
    b, _, t, _ = q_cat.shape
    return pl.pallas_call(
        functools.partial(_attn_prompt_kernel, tq=tq),
        grid=(b, t // tq),
        in_specs=[
            pl.BlockSpec((None, MLA_HEADS, tq, HEAD_W), lambda i, j: (i, 0, j, 0)),
            pl.BlockSpec((None, MLA_HEADS, t, HEAD_W), lambda i, j: (i, 0, 0, 0)),
            pl.BlockSpec((None, t, MLA_KV_LORA), lambda i, j: (i, 0, 0)),
            pl.BlockSpec((None, MLA_HEADS, MLA_KV_LORA, MLA_HEADS * MLA_V), lambda i, j: (layer, 0, 0, 0)),
        ],
        out_specs=pl.BlockSpec((None, tq, MLA_HEADS * MLA_V), lambda i, j: (i, j, 0)),
        out_shape=jax.ShapeDtypeStruct((b, t, MLA_HEADS * MLA_V), F32),
        compiler_params=_params(("arbitrary", "arbitrary"), 48),
        name="attn_prompt",
    )(q_cat, k_cat, ckv_bf, w_uvp)


def _attn_sample_kernel(pt_ref, q_ref, ckvn_ref, krn_ref, wabs_ref, wuk_ref, segrep_ref, wuv_ref,
                        cache_ckv, cache_kr, o_ref, ckv_buf, kr_buf, s_buf, sem,
                        *, layer, n_pages, chunk, ts):
    b = pl.program_id(0)
    nb = pl.num_programs(0)
    past = n_pages * PAGE_SIZE
    hq = MLA_HEADS * ts

    def page_copies(seq, slot, j):
        page = pt_ref[seq, j]
        dst = pl.ds(j * PAGE_SIZE, PAGE_SIZE)
        return (pltpu.make_async_copy(cache_ckv.at[layer, page], ckv_buf.at[slot, dst, :], sem.at[slot, 0]),
                pltpu.make_async_copy(cache_kr.at[layer, page], kr_buf.at[slot, dst, :], sem.at[slot, 1]))

    def start_seq(seq, slot):
        def body(j, _):
            c0, c1 = page_copies(seq, slot, j)
            c0.start()
            c1.start()
            return 0
        lax.fori_loop(0, n_pages, body, 0)

    def wait_seq(seq, slot):
        def body(j, _):
            c0, c1 = page_copies(seq, slot, j)
            c0.wait()
            c1.wait()
            return 0
        lax.fori_loop(0, n_pages, body, 0)

    slot = lax.rem(b, 2)

    @pl.when(b == 0)
    def _():
        start_seq(0, 0)

    @pl.when(b + 1 < nb)
    def _():
        start_seq(b + 1, 1 - slot)

    wait_seq(b, slot)

    q_all = q_ref[...].reshape(hq, HEAD_W)
    q_wide = _dot(q_all.astype(BF16), wabs_ref[...])
    q_abs = jnp.concatenate(
        [q_wide[h * ts:(h + 1) * ts, h * MLA_KV_LORA:(h + 1) * MLA_KV_LORA] for h in range(MLA_HEADS)],
        axis=0).astype(BF16)
    q_rope = q_all[:, MLA_NOPE:MLA_NOPE + MLA_ROPE].astype(BF16)

    def scores(ckv_rows, kr_rows):
        c = ckv_rows.astype(BF16)
        k = _dot(c, wuk_ref[...])
        ssq = _dot((k * k).astype(BF16), segrep_ref[...])
        r = lax.rsqrt(ssq * (1.0 / MLA_NOPE) + EPS)
        return _dot_nt(c, q_abs) * r + _dot_nt(kr_rows.astype(BF16), q_rope)

    n_chunks = past // chunk

    def pass1(ci, m):
        off = pl.multiple_of(ci * chunk, chunk)
        s = scores(ckv_buf[slot, pl.ds(off, chunk), :], kr_buf[slot, pl.ds(off, chunk), :])
        s_buf[pl.ds(off, chunk), :] = s
        return jnp.maximum(m, jnp.max(s, axis=0, keepdims=True))

    m = lax.fori_loop(0, n_chunks, pass1, jnp.full((1, hq), -jnp.inf, F32))
    s_new = scores(ckvn_ref[...], krn_ref[...])
    jrow = lax.broadcasted_iota(jnp.int32, (ts, hq), 0)
    tcol = lax.rem(lax.broadcasted_iota(jnp.int32, (ts, hq), 1), ts)
    s_new = jnp.where(jrow <= tcol, s_new, -jnp.inf)
    m = jnp.maximum(m, jnp.max(s_new, axis=0, keepdims=True))

    ones_c = jnp.ones((chunk, MLA_KV_LORA), BF16)

    def pass2(ci, acc):
        off = pl.multiple_of(ci * chunk, chunk)
        p = jnp.exp(s_buf[pl.ds(off, chunk), :] - m).astype(BF16)
        v = jnp.concatenate([ckv_buf[slot, pl.ds(off, chunk), :].astype(BF16), ones_c], axis=1)
        return acc + _dot_tn(p, v)

    acc = lax.fori_loop(0, n_chunks, pass2, jnp.zeros((hq, 2 * MLA_KV_LORA), F32))
    p_new = jnp.exp(s_new - m).astype(BF16)
    v_new = jnp.concatenate([ckvn_ref[...].astype(BF16), jnp.ones((ts, MLA_KV_LORA), BF16)], axis=1)
    acc = acc + _dot_tn(p_new, v_new)
    o = (acc[:, 0:MLA_KV_LORA] / acc[:, MLA_KV_LORA:]).astype(BF16)
    y = jnp.zeros((ts, MLA_HEADS * MLA_V), F32)
    for h in range(MLA_HEADS):
        y = y + _dot(o, wuv_ref[h])[h * ts:(h + 1) * ts, :]
    o_ref[...] = y


def _attn_sample_call(page_table, q_cat, ckv_new, kr_new, wp, cache_ckv, cache_kr, layer, chunk):
    b, _, ts, _ = q_cat.shape
    n_pages = page_table.shape[1]
    past = n_pages * PAGE_SIZE
    hq = MLA_HEADS * ts
    kern = functools.partial(_attn_sample_kernel, layer=layer, n_pages=n_pages, chunk=chunk, ts=ts)
    lay3 = lambda arr: pl.BlockSpec((None,) + arr.shape[1:], lambda i, pt: (layer,) + (0,) * (arr.ndim - 1))
    grid_spec = pltpu.PrefetchScalarGridSpec(
        num_scalar_prefetch=1,
        grid=(b,),
        in_specs=[
            pl.BlockSpec((None, MLA_HEADS, ts, HEAD_W), lambda i, pt: (i, 0, 0, 0)),
            pl.BlockSpec((None, ts, MLA_KV_LORA), lambda i, pt: (i, 0, 0)),
            pl.BlockSpec((None, ts, MLA_ROPE), lambda i, pt: (i, 0, 0)),
            lay3(wp["w_abs"]), lay3(wp["w_uk_c"]),
            pl.BlockSpec(wp["seg_rep"].shape, lambda i, pt: (0, 0)),
            lay3(wp["w_uvp"]),
            pl.BlockSpec(memory_space=pl.ANY),
            pl.BlockSpec(memory_space=pl.ANY),
        ],
        out_specs=pl.BlockSpec((None, ts, MLA_HEADS * MLA_V), lambda i, pt: (i, 0, 0)),
        scratch_shapes=[
            pltpu.VMEM((2, past, MLA_KV_LORA), F32),
            pltpu.VMEM((2, past, MLA_ROPE), F32),
            pltpu.VMEM((past, hq), F32),
            pltpu.SemaphoreType.DMA((2, 2)),
        ],
    )
    return pl.pallas_call(
        kern,
        grid_spec=grid_spec,
        out_shape=jax.ShapeDtypeStruct((b, ts, MLA_HEADS * MLA_V), F32),
        compiler_params=_params(("arbitrary",), 56),
        name="attn_sample",
    )(page_table, q_cat, ckv_new, kr_new, wp["w_abs"], wp["w_uk_c"], wp["seg_rep"], wp["w_uvp"],
      cache_ckv, cache_kr)


def _cumsum_rows(x):
    n = x.shape[0]
    row = lax.broadcasted_iota(jnp.int32, x.shape, 0)
    d = 1
    while d < n:
        x = x + jnp.where(row >= d, pltpu.roll(x, d, 0), 0.0)
        d *= 2
    return x


def _seq_kernel(conv_ref, gla_ref, cprev_ref, s0_ref, wconv_ref, wa2_ref, ba_ref, go_ref,
                segexp_ref, bdmask_ref, p256_ref,
                y_ref, cstate_ref, sstate_ref, s_scr, c_scr, v_scr):
    ci = pl.program_id(1)
    nci = pl.num_programs(1)
    ln = conv_ref.shape[0]
    kd = GLA_HEADS * GLA_DK
    vd = GLA_HEADS * GLA_DV

    @pl.when(ci == 0)
    def _():
        c_scr[...] = cprev_ref[...]
        s_scr[...] = jnp.zeros(s_scr.shape, F32)
        for h in range(GLA_HEADS):
            s_scr[h * GLA_DK:(h + 1) * GLA_DK, h * GLA_DV:(h + 1) * GLA_DV] = s0_ref[h]

    conv = conv_ref[...]
    v = conv[:, CONV_DIM:2 * CONV_DIM] * conv[:, 0:CONV_DIM]
    row = lax.broadcasted_iota(jnp.int32, (ln, CONV_DIM), 0)
    prev = c_scr[...]
    p0 = jnp.broadcast_to(prev[0:1, :], (ln, CONV_DIM))
    p1 = jnp.broadcast_to(prev[1:2, :], (ln, CONV_DIM))
    v1 = jnp.where(row == 0, p1, pltpu.roll(v, 1, 0))
    v2 = jnp.where(row == 0, p0, jnp.where(row == 1, p1, pltpu.roll(v, 2, 0)))
    wc = wconv_ref[...]
    y_a = conv[:, 2 * CONV_DIM:] * (v2 * wc[0:1, :] + v1 * wc[1:2, :] + v * wc[2:3, :])
    v_scr[...] = v
    c_scr[...] = v_scr[ln - 2:ln, :]
    y_ref[:, 0:CONV_DIM] = y_a

    gla = gla_ref[...]
    gq = gla[:, 0:kd] * (GLA_DK ** -0.5)
    gk = gla[:, kd:2 * kd]
    gv = gla[:, 2 * kd:2 * kd + vd]
    ga = gla[:, 2 * kd + vd:2 * kd + vd + LANES]
    gr = gla[:, 2 * kd + vd + LANES:]
    x = _dot(ga.astype(BF16), wa2_ref[...]) + ba_ref[...]
    log_a = (jnp.minimum(x, 0.0) - jnp.log(1.0 + jnp.exp(-jnp.abs(x)))) * (1.0 / GLA_NORMALIZER)
    cum = _cumsum_rows(log_a)
    s_prev = s_scr[...]
    o = _dot((gq * jnp.exp(cum)).astype(BF16), s_prev.astype(BF16))
    segexp = segexp_ref[...]
    rowk = lax.broadcasted_iota(jnp.int32, (ln, kd), 0)
    o_intra = [jnp.zeros((ln - g0, vd), F32) for g0 in range(0, ln, SUBLANES)]
    for s in range(ln):
        g = s // SUBLANES
        g0 = g * SUBLANES
        dec = jnp.exp(jnp.minimum(cum[g0:, :] - cum[s:s + 1, :], 0.0))
        dmat = jnp.where(rowk[g0:, :] >= s, gq[g0:, :] * gk[s:s + 1, :] * dec, 0.0)
        att = _dot(dmat.astype(BF16), segexp)
        o_intra[g] = o_intra[g] + att * gv[s:s + 1, :]
    intra = jnp.zeros((ln, vd), F32)
    for g, part in enumerate(o_intra):
        g0 = g * SUBLANES
        if g0 == 0:
            intra = intra + part
        else:
            intra = intra + jnp.concatenate([jnp.zeros((g0, vd), F32), part], axis=0)
    o = o + intra
    last = cum[ln - 1:ln, :]
    kdec = gk * jnp.exp(last - cum)
    e_col = jnp.transpose(jnp.broadcast_to(jnp.exp(last), (kd, kd)))
    e_col = jnp.concatenate([e_col, e_col], axis=1)
    s_new = e_col * s_prev + bdmask_ref[...] * _dot_tn(kdec.astype(BF16), gv.astype(BF16))
    s_scr[...] = s_new
    rstd = lax.rsqrt(_dot((o * o).astype(BF16), p256_ref[...]) * (1.0 / GLA_DV) + EPS)
    y_ref[:, CONV_DIM:] = o * rstd * go_ref[...] * (gr * _sigmoid(gr))

    @pl.when(ci == nci - 1)
    def _():
        cstate_ref[...] = c_scr[...]
        for h in range(GLA_HEADS):
            sstate_ref[h] = s_scr[h * GLA_DK:(h + 1) * GLA_DK, h * GLA_DV:(h + 1) * GLA_DV]


def _seq_call(conv, gla, conv_prev, s0, wp, layer, ln):
    b, t, _ = conv.shape
    kd = GLA_HEADS * GLA_DK
    vd = GLA_HEADS * GLA_DV
    lay = lambda arr: pl.BlockSpec((None,) + arr.shape[1:], lambda i, j: (layer,) + (0,) * (arr.ndim - 1))
    const2 = lambda arr: pl.BlockSpec(arr.shape, lambda i, j: (0, 0))
    return pl.pallas_call(
        _seq_kernel,
        grid=(b, t // ln),
        in_specs=[
            pl.BlockSpec((None, ln, W_CONV_OUT), lambda i, j: (i, j, 0)),
            pl.BlockSpec((None, ln, W_GLA_OUT), lambda i, j: (i, j, 0)),
            pl.BlockSpec((None, CONV_W - 1, CONV_DIM), lambda i, j: (i, 0, 0)),
            pl.BlockSpec((None, GLA_HEADS, GLA_DK, GLA_DV), lambda i, j: (i, 0, 0, 0)),
            lay(wp["w_conv"]), lay(wp["w_a2p"]), lay(wp["b_a"]), lay(wp["g_gla_lane"]),
            const2(wp["seg_exp"]), const2(wp["bd_mask"]), const2(wp["p256"]),
        ],
        out_specs=[
            pl.BlockSpec((None, ln, CONV_DIM + vd), lambda i, j: (i, j, 0)),
            pl.BlockSpec((None, CONV_W - 1, CONV_DIM), lambda i, j: (i, 0, 0)),
            pl.BlockSpec((None, GLA_HEADS, GLA_DK, GLA_DV), lambda i, j: (i, 0, 0, 0)),
        ],
        out_shape=[jax.ShapeDtypeStruct((b, t, CONV_DIM + vd), F32),
                   jax.ShapeDtypeStruct((b, CONV_W - 1, CONV_DIM), F32),
                   jax.ShapeDtypeStruct((b, GLA_HEADS, GLA_DK, GLA_DV), F32)],
        scratch_shapes=[pltpu.VMEM((kd, vd), F32), pltpu.VMEM((CONV_W - 1, CONV_DIM), F32),
                        pltpu.VMEM((ln, CONV_DIM), F32)],
        compiler_params=_params(("arbitrary", "arbitrary"), 40),
        name="conv_gla",
    )(conv, gla, conv_prev, s0, wp["w_conv"], wp["w_a2p"], wp["b_a"], wp["g_gla_lane"],
      wp["seg_exp"], wp["bd_mask"], wp["p256"])


def _split_bf16(x):
    hi = x.astype(BF16)
    lo = (x - hi.astype(F32)).astype(BF16)
    return hi, lo


def _out_kernel(yac_ref, yb_ref, x_ref, mod_ref, woac_ref, wob_ref, g_ref, wrh_ref, wrl_ref, br_ref,
                x1_ref, h_ref, idx_ref, gate_ref):
    bb, tt, _ = x_ref.shape
    rows = bb * tt
    yac = yac_ref[...].reshape(rows, 2 * CONV_DIM).astype(BF16)
    yb = yb_ref[...].reshape(rows, MLA_HEADS * MLA_V).astype(BF16)
    y = (_dot(yac, woac_ref[...]) + _dot(yb, wob_ref[...])).reshape(bb, tt, D_MODEL)
    x1 = x_ref[...] + mod_ref[:, 2:3, :] * y
    x1_ref[...] = x1
    ms = jnp.mean(x1 * x1, axis=-1, keepdims=True)
    h = x1 * lax.rsqrt(ms + EPS) * g_ref[...]
    h = (h * (1.0 + mod_ref[:, 4:5, :]) + mod_ref[:, 3:4, :]).reshape(rows, D_MODEL)
    h_ref[...] = h
    h_hi, h_lo = _split_bf16(h)
    logits = _dot(h_hi, wrh_ref[...]) + (_dot(h_hi, wrl_ref[...]) + _dot(h_lo, wrh_ref[...])) + br_ref[...]
    lane = lax.broadcasted_iota(jnp.int32, (rows, LANES), 1)
    lane_f = lane.astype(F32)
    idx_out = jnp.zeros((rows, LANES), F32)
    val_out = jnp.zeros((rows, LANES), F32)
    vals = []
    for k in range(TOP_K):
        mx = jnp.max(logits, axis=-1, keepdims=True)
        ix = jnp.min(jnp.where(logits == mx, lane_f, float(LANES)), axis=-1, keepdims=True)
        idx_out = jnp.where(lane == k, ix, idx_out)
        vals.append(mx)
        logits = jnp.where(lane_f == ix, -jnp.inf, logits)
    es = [jnp.exp(v - vals[0]) for v in vals]
    den = es[0] + es[1] + es[2] + es[3]
    for k in range(TOP_K):
        val_out = jnp.where(lane == k, es[k] / den, val_out)
    idx_ref[...] = idx_out.astype(jnp.int32)
    gate_ref[...] = val_out


def _out_call(yac, yb, x, mod, wp, layer, bb, tt):
    b, t, _ = x.shape
    n = b * t
    rows = bb * tt
    nj = t // tt
    bs3 = lambda w: pl.BlockSpec((bb, tt, w), lambda i, j: (i, j, 0))
    lay = lambda arr: pl.BlockSpec((None,) + arr.shape[1:], lambda i, j: (layer,) + (0,) * (arr.ndim - 1))
    flat = lambda w: pl.BlockSpec((rows, w), lambda i, j: (i * nj + j, 0))
    return pl.pallas_call(
        _out_kernel,
        grid=(b // bb, nj),
        in_specs=[bs3(2 * CONV_DIM), bs3(MLA_HEADS * MLA_V), bs3(D_MODEL),
                  pl.BlockSpec((bb, N_MOD, D_MODEL), lambda i, j: (i, 0, 0)),
                  lay(wp["w_o_ac"]), lay(wp["w_o_b"]), lay(wp["g_ffn"]),
                  lay(wp["w_r_hi"]), lay(wp["w_r_lo"]), lay(wp["b_r"])],
        out_specs=[bs3(D_MODEL), flat(D_MODEL), flat(LANES), flat(LANES)],
        out_shape=[jax.ShapeDtypeStruct((b, t, D_MODEL), F32),
                   jax.ShapeDtypeStruct((n, D_MODEL), F32),
                   jax.ShapeDtypeStruct((n, LANES), jnp.int32),
                   jax.ShapeDtypeStruct((n, LANES), F32)],
        compiler_params=_params(("arbitrary", "arbitrary"), 48),
        name="out_proj_router",
    )(yac, yb, x, mod, wp["w_o_ac"], wp["w_o_b"], wp["g_ffn"], wp["w_r_hi"], wp["w_r_lo"], wp["b_r"])


def _dispatch_kernel(pos_ref, padstart_ref, padn_ref, h_hbm, zero_ref, xs_hbm, sem, *, n_steps):
    i = pl.program_id(0)

    @pl.when(i < n_steps)
    def _():
        base = i * DISPATCH_ROWS

        def copy(r):
            tok = lax.shift_right_logical(base + r, 2)
            return pltpu.make_async_copy(h_hbm.at[tok], xs_hbm.at[pos_ref[0, 0, r]], sem.at[0])

        def start(r, _):
            copy(r).start()
            return 0

        def wait(r, _):
            copy(r).wait()
            return 0

        lax.fori_loop(0, DISPATCH_ROWS, start, 0)
        lax.fori_loop(0, DISPATCH_ROWS, wait, 0)

    @pl.when(i == n_steps)
    def _():
        def copy(e, r):
            return pltpu.make_async_copy(zero_ref.at[0], xs_hbm.at[padstart_ref[e] + r], sem.at[1])

        def for_all(action):
            def per_expert(e, _):
                lax.fori_loop(0, padn_ref[e], lambda r, c: action(e, r), 0)
                return 0
            lax.fori_loop(0, N_EXPERTS, per_expert, 0)

        def start(e, r):
            copy(e, r).start()
            return 0

        def wait(e, r):
            copy(e, r).wait()
            return 0

        for_all(start)
        for_all(wait)


def _dispatch_call(pos, padstart, padn, h_all, p_rows):
    m = pos.shape[0]
    n_steps = m // DISPATCH_ROWS
    pos3 = pos.reshape(n_steps, 1, DISPATCH_ROWS)
    return pl.pallas_call(
        functools.partial(_dispatch_kernel, n_steps=n_steps),
        grid=(n_steps + 1,),
        in_specs=[
            pl.BlockSpec((1, 1, DISPATCH_ROWS), lambda i: (jnp.minimum(i, n_steps - 1), 0, 0),
                         memory_space=pltpu.SMEM),
            pl.BlockSpec(memory_space=pltpu.SMEM),
            pl.BlockSpec(memory_space=pltpu.SMEM),
            pl.BlockSpec(memory_space=pl.ANY),
            pl.BlockSpec((SUBLANES, D_MODEL), lambda i: (0, 0)),
        ],
        out_specs=pl.BlockSpec(memory_space=pl.ANY),
        out_shape=jax.ShapeDtypeStruct((p_rows, D_MODEL), F32),
        scratch_shapes=[pltpu.SemaphoreType.DMA((2,))],
        compiler_params=_params(("arbitrary",), 32),
        name="moe_dispatch",
    )(pos3, padstart, padn, h_all, jnp.zeros((SUBLANES, D_MODEL), F32))


def _experts_kernel(be_ref, nv_ref, x_ref, wg_ref, bg_ref, wu_ref, bu_ref, wd_ref, bd_ref, o_ref,
                    wg_s, wu_s, wd_s):
    i = pl.program_id(0)

    @pl.when(i < nv_ref[0])
    def _():
        prev = be_ref[jnp.maximum(i - 1, 0)]

        @pl.when((i == 0) | (be_ref[i] != prev))
        def _():
            wg_s[...] = wg_ref[...].astype(BF16)
            wu_s[...] = wu_ref[...].astype(BF16)
            wd_s[...] = wd_ref[...].astype(BF16)

        x = x_ref[...].astype(BF16)
        g = _dot(x, wg_s[...]) + bg_ref[...]
        u = _dot(x, wu_s[...]) + bu_ref[...]
        g = jnp.minimum(g, SWIGLU_LIMIT)
        u = jnp.clip(u, -SWIGLU_LIMIT, SWIGLU_LIMIT)
        a = (u + 1.0) * (g * _sigmoid(SWIGLU_ALPHA * g))
        o_ref[...] = _dot(a.astype(BF16), wd_s[...]) + bd_ref[...]


def _experts_call(blk_e, n_valid, xs, w_gate, b_gate, w_up, b_up, w_down, b_down, layer):
    p_rows = xs.shape[0]
    nblk = p_rows // MOE_TM
    eff = lambda i, nv: jnp.minimum(i, nv[0] - 1)
    wspec = pl.BlockSpec((None, None, D_MODEL, D_FF), lambda i, be, nv: (layer, be[eff(i, nv)], 0, 0))
    bspec = pl.BlockSpec((None, None, 1, D_FF), lambda i, be, nv: (layer, be[eff(i, nv)], 0, 0))
    rows = pl.BlockSpec((MOE_TM, D_MODEL), lambda i, be, nv: (eff(i, nv), 0))
    grid_spec = pltpu.PrefetchScalarGridSpec(
        num_scalar_prefetch=2,
        grid=(nblk,),
        in_specs=[rows, wspec, bspec, wspec, bspec, wspec, bspec],
        out_specs=rows,
        scratch_shapes=[pltpu.VMEM((D_MODEL, D_FF), BF16), pltpu.VMEM((D_MODEL, D_FF), BF16),
                        pltpu.VMEM((D_FF, D_MODEL), BF16)],
    )
    depth = w_gate.shape[0]
    b4 = lambda bias: bias.reshape(depth, N_EXPERTS, 1, bias.shape[-1])
    return pl.pallas_call(
        _experts_kernel,
        grid_spec=grid_spec,
        out_shape=jax.ShapeDtypeStruct((p_rows, D_MODEL), F32),
        compiler_params=_params(("arbitrary",), 56),
        name="moe_experts",
    )(blk_e, n_valid, xs, w_gate, b4(b_gate), w_up, b4(b_up), w_down, b4(b_down))


def _combine_kernel(pos_ref, gate_ref, x1_ref, mod_ref, yb_hbm, o_ref, buf, sem):
    bb, tt, _ = x1_ref.shape
    rows = bb * tt

    def copy(r):
        k = lax.rem(r, TOP_K)
        t = lax.shift_right_logical(r, 2)
        return pltpu.make_async_copy(yb_hbm.at[pos_ref[0, 0, r]], buf.at[k, t], sem.at[0])

    def start(r, _):
        copy(r).start()
        return 0

    def wait(r, _):
        copy(r).wait()
        return 0

    lax.fori_loop(0, rows * TOP_K, start, 0)
    lax.fori_loop(0, rows * TOP_K, wait, 0)
    gates = gate_ref[...]
    y = buf[0] * gates[:, 0:1]
    for k in range(1, TOP_K):
        y = y + buf[k] * gates[:, k:k + 1]
    o_ref[...] = x1_ref[...] + mod_ref[:, 5:6, :] * y.reshape(bb, tt, D_MODEL)


def _combine_call(pos, gates, x1, mod, yb, tok0, bb, tt):
    b, t, _ = x1.shape
    rows = bb * tt
    nj = t // tt
    blk0 = tok0 // rows
    n_all = pos.shape[0] // TOP_K
    pos3 = pos.reshape(n_all // rows, 1, rows * TOP_K)
    return pl.pallas_call(
        _combine_kernel,
        grid=(b // bb, nj),
        in_specs=[
            pl.BlockSpec((1, 1, rows * TOP_K), lambda i, j: (blk0 + i * nj + j, 0, 0), memory_space=pltpu.SMEM),
            pl.BlockSpec((rows, LANES), lambda i, j: (blk0 + i * nj + j, 0)),
            pl.BlockSpec((bb, tt, D_MODEL), lambda i, j: (i, j, 0)),
            pl.BlockSpec((bb, N_MOD, D_MODEL), lambda i, j: (i, 0, 0)),
            pl.BlockSpec(memory_space=pl.ANY),
        ],
        out_specs=pl.BlockSpec((bb, tt, D_MODEL), lambda i, j: (i, j, 0)),
        out_shape=jax.ShapeDtypeStruct((b, t, D_MODEL), F32),
        scratch_shapes=[pltpu.VMEM((TOP_K, rows, D_MODEL), F32), pltpu.SemaphoreType.DMA((1,))],
        compiler_params=_params(("arbitrary", "arbitrary"), 32),
        name="moe_combine",
    )(pos3, gates, x1, mod, yb)


def _route(top_i, n_tokens):
    m = n_tokens * TOP_K
    flat_e = top_i.reshape(m)
    onehot = (flat_e[:, None] == jnp.arange(N_EXPERTS, dtype=jnp.int32)[None, :]).astype(jnp.int32)
    csum = jnp.cumsum(onehot, axis=0)
    rank = jnp.sum(onehot * csum, axis=1) - 1
    counts = csum[-1]
    padded = (counts + MOE_TM - 1) // MOE_TM * MOE_TM
    pends = jnp.cumsum(padded)
    pstarts = pends - padded
    pos = jnp.sum(onehot * pstarts[None, :], axis=1) + rank
    p_rows = (m // MOE_TM + N_EXPERTS) * MOE_TM
    nblk = p_rows // MOE_TM
    blk_start = jnp.arange(nblk, dtype=jnp.int32) * MOE_TM
    blk_e = jnp.minimum(jnp.sum((blk_start[:, None] >= pends[None, :]).astype(jnp.int32), axis=1), N_EXPERTS - 1)
    n_valid = (pends[-1] // MOE_TM).reshape(1)
    route = dict(pos=pos, blk_e=blk_e, n_valid=n_valid, padstart=pstarts + counts, padn=padded - counts)
    return {k: v.astype(jnp.int32) for k, v in route.items()}, p_rows


def _prep_weights(g_mix, g_ffn, w_in, w_o, w_conv, g_qa, w_uq, g_qn, g_qr, g_kva, g_kr, w_uk, g_kn, w_uv,
                  w_a2, b_a, g_gla_o, w_router, b_router):
    depth = w_in.shape[0]
    offs = np.cumsum((0,) + IN_SIZES)
    seg = lambda i: w_in[:, :, offs[i]:offs[i + 1]]
    z = lambda n: jnp.zeros((depth, D_MODEL, n), w_in.dtype)
    w_in_p = jnp.concatenate(
        [seg(0), seg(1), seg(2), seg(3), seg(4), z(64), seg(5), z(32),
         seg(6), seg(7), seg(8), seg(9), z(LANES - GLA_GATE_RANK), seg(10)], axis=-1).astype(BF16)
    pad_last = lambda a, n: jnp.pad(a, [(0, 0)] * (a.ndim - 1) + [(0, n - a.shape[-1])])
    v3 = lambda a: a.reshape(depth, 1, a.shape[-1])
    wp = {}
    wp["g_mix"] = v3(g_mix)
    wp["g_ffn"] = v3(g_ffn)
    wp["w_in_p"] = w_in_p
    wp["g_qa"] = v3(g_qa)
    wp["w_uq"] = pad_last(w_uq, HEAD_W).reshape(depth, MLA_Q_LORA, MLA_HEADS * HEAD_W).astype(BF16)
    wp["gq_lane"] = v3(pad_last(jnp.concatenate([g_qn, g_qr], axis=-1), HEAD_W))
    wp["g_kva"] = v3(g_kva)
    wp["w_uk"] = pad_last(w_uk, HEAD_W).reshape(depth, MLA_KV_LORA, MLA_HEADS * HEAD_W).astype(BF16)
    wp["w_uk_c"] = w_uk.reshape(depth, MLA_KV_LORA, MLA_HEADS * MLA_NOPE).astype(BF16)
    wp["gk_lane"] = v3(pad_last(g_kn, HEAD_W))
    wp["gkr_lane"] = v3(pad_last(jnp.concatenate([jnp.zeros((depth, MLA_NOPE), F32), g_kr], axis=-1), HEAD_W))
    w_abs = jnp.transpose(w_uk, (0, 3, 2, 1)) * g_kn[:, :, None, None]
    w_abs = w_abs.reshape(depth, MLA_NOPE, MLA_HEADS * MLA_KV_LORA)
    wp["w_abs"] = jnp.pad(w_abs, ((0, 0), (0, HEAD_W - MLA_NOPE), (0, 0))).astype(BF16)
    eye_h = jnp.eye(MLA_HEADS, dtype=F32)
    wp["w_uvp"] = jnp.einsum("lrhv,hg->lhrgv", w_uv, eye_h).reshape(
        depth, MLA_HEADS, MLA_KV_LORA, MLA_HEADS * MLA_V).astype(BF16)
    wp["w_o_ac"] = jnp.concatenate([w_o[:, 0:CONV_DIM], w_o[:, CONV_DIM + MLA_HEADS * MLA_V:]], axis=1).astype(BF16)
    wp["w_o_b"] = w_o[:, CONV_DIM:CONV_DIM + MLA_HEADS * MLA_V].astype(BF16)
    wp["w_conv"] = w_conv
    wp["w_a2p"] = jnp.pad(w_a2, ((0, 0), (0, LANES - GLA_GATE_RANK), (0, 0))).astype(BF16)
    wp["b_a"] = v3(b_a)
    wp["g_gla_lane"] = v3(jnp.tile(g_gla_o, (1, GLA_HEADS)))
    w_r = pad_last(w_router, LANES)
    w_r_hi = w_r.astype(BF16)
    wp["w_r_hi"] = w_r_hi
    wp["w_r_lo"] = (w_r - w_r_hi.astype(F32)).astype(BF16)
    wp["b_r"] = v3(jnp.concatenate([b_router, jnp.full((depth, LANES - N_EXPERTS), NEG_BIG, F32)], axis=-1))
    lane = np.arange(LANES)
    seg_id = np.where(lane < MLA_NOPE, 0, np.where(lane < MLA_NOPE + MLA_ROPE, 1, 2 + lane))
    wp["p_seg"] = jnp.asarray((seg_id[:, None] == seg_id[None, :]).astype(np.float32), BF16)
    wp["invn"] = jnp.asarray(np.where(lane < MLA_NOPE, 1.0 / MLA_NOPE, 1.0 / MLA_ROPE).astype(np.float32))[None, :]
    kd = GLA_HEADS * GLA_DK
    vd = GLA_HEADS * GLA_DV
    hk = np.arange(kd) // GLA_DK
    hv = np.arange(vd) // GLA_DV
    bd = (hk[:, None] == hv[None, :]).astype(np.float32)
    wp["seg_exp"] = jnp.asarray(bd, BF16)
    wp["bd_mask"] = jnp.asarray(bd, F32)
    wp["p256"] = jnp.asarray((hv[:, None] == hv[None, :]).astype(np.float32), BF16)
    return wp


def _rope_tables(pos):
    half = MLA_ROPE // 2
    freqs = ROPE_THETA ** (-jnp.arange(half, dtype=F32) / half)
    ang = pos.astype(F32)[:, None] * freqs
    cos, sin = jnp.cos(ang), jnp.sin(ang)
    t = pos.shape[0]
    cos_t = jnp.concatenate([jnp.ones((t, MLA_NOPE), F32), cos, cos, jnp.zeros((t, HEAD_W - 96), F32)], axis=-1)
    sin_t = jnp.concatenate([jnp.zeros((t, MLA_NOPE), F32), -sin, sin, jnp.zeros((t, HEAD_W - 96), F32)], axis=-1)
    return cos_t, sin_t


def kernel(x_prompt, x_sample, cache_ckv, cache_kr, state_conv, state_gla, page_table, c_prompt, c_sample, g_mix, g_ffn, w_mod, b_mod, w_in, w_o, w_conv, g_qa, w_uq, g_qn, g_qr, g_kva, g_kr, w_uk, g_kn, w_uv, w_a2, b_a, g_gla_o, w_router, b_router, w_gate, b_gate, w_up, b_up, w_down, b_down):
    depth = w_in.shape[0]
    bp, tp, _ = x_prompt.shape
    bs, ts, _ = x_sample.shape
    n_pages = page_table.shape[1]
    past = n_pages * PAGE_SIZE
    n_p, n_s = bp * tp, bs * ts
    n_all = n_p + n_s

    tt_p = min(512, tp)
    bb_s = min(16, bs)
    tq = min(256, tp)
    ln_p = min(64, tp)
    chunk_s = min(1024, past)
    rows_c = COMBINE_TOK

    wp = _prep_weights(g_mix, g_ffn, w_in, w_o, w_conv, g_qa, w_uq, g_qn, g_qr, g_kva, g_kr, w_uk, g_kn, w_uv,
                       w_a2, b_a, g_gla_o, w_router, b_router)
    hq = MLA_HEADS * ts
    seg_rep = (np.arange(MLA_HEADS * MLA_NOPE)[:, None] // MLA_NOPE == np.arange(hq)[None, :] // ts)
    wp["seg_rep"] = jnp.asarray(seg_rep.astype(np.float32), BF16)

    nb_mod = -(-(bp + bs) // 16) * 16
    c_all = jnp.concatenate([c_prompt, c_sample, jnp.zeros((nb_mod - bp - bs, D_MODEL), F32)], axis=0)
    mod_all = _mod_call(c_all, w_mod, b_mod)

    cos_p, sin_p = _rope_tables(jnp.arange(tp, dtype=jnp.int32))
    cos_s, sin_s = _rope_tables(past + jnp.arange(ts, dtype=jnp.int32))
    cos_s, sin_s = jnp.tile(cos_s, (bb_s, 1)), jnp.tile(sin_s, (bb_s, 1))

    conv0 = jnp.zeros((bp, CONV_W - 1, CONV_DIM), F32)
    gla0 = jnp.zeros((bp, GLA_HEADS, GLA_DK, GLA_DV), F32)

    xp, xs = x_prompt, x_sample
    outs = {k: [] for k in ("ckv_p", "kr_p", "conv_p", "gla_p", "ckv_s", "kr_s", "conv_s", "gla_s")}
    for l in range(depth):
        mod_p = mod_all[l, 0:bp].reshape(bp, N_MOD, D_MODEL)
        mod_s = mod_all[l, bp:bp + bs].reshape(bs, N_MOD, D_MODEL)

        conv_o, mla_o, gla_o = _in_call(xp, mod_p, wp["g_mix"], wp["w_in_p"], l, 1, tt_p)
        q_cat, k_cat, ckv, kr128 = _qk_call(mla_o, cos_p, sin_p, lambda i, j: j, wp, l, 1, tt_p, BF16)
        yb_p = _attn_prompt_call(q_cat, k_cat, ckv.astype(BF16), wp["w_uvp"], l, tq)
        yac_p, conv_st, gla_st = _seq_call(conv_o, gla_o, conv0, gla0, wp, l, ln_p)
        outs["ckv_p"].append(ckv)
        outs["kr_p"].append(kr128[:, :, MLA_NOPE:MLA_NOPE + MLA_ROPE])
        outs["conv_p"].append(conv_st)
        outs["gla_p"].append(gla_st)
        x1_p, h_p, idx_p, gate_p = _out_call(yac_p, yb_p, xp, mod_p, wp, l, 1, tt_p)

        conv_o, mla_o, gla_o = _in_call(xs, mod_s, wp["g_mix"], wp["w_in_p"], l, bb_s, ts)
        q_cat, _, ckv, kr128 = _qk_call(mla_o, cos_s, sin_s, lambda i, j: 0, wp, l, bb_s, ts, F32)
        kr_s = kr128[:, :, MLA_NOPE:MLA_NOPE + MLA_ROPE]
        yb_s = _attn_sample_call(page_table, q_cat, ckv, kr_s, wp, cache_ckv, cache_kr, l, chunk_s)
        yac_s, conv_st, gla_st = _seq_call(conv_o, gla_o, state_conv[l], state_gla[l], wp, l, ts)
        outs["ckv_s"].append(ckv)
        outs["kr_s"].append(kr_s)
        outs["conv_s"].append(conv_st)
        outs["gla_s"].append(gla_st)
        x1_s, h_s, idx_s, gate_s = _out_call(yac_s, yb_s, xs, mod_s, wp, l, bb_s, ts)

        h_all = jnp.concatenate([h_p, h_s], axis=0)
        top_i = jnp.concatenate([idx_p[:, 0:TOP_K], idx_s[:, 0:TOP_K]], axis=0)
        gates = jnp.concatenate([gate_p, gate_s], axis=0)
        rt, p_rows = _route(top_i, n_all)
        x_sorted = _dispatch_call(rt["pos"], rt["padstart"], rt["padn"], h_all, p_rows)
        y_sorted = _experts_call(rt["blk_e"], rt["n_valid"], x_sorted, w_gate, b_gate, w_up, b_up, w_down, b_down, l)
        xp = _combine_call(rt["pos"], gates, x1_p, mod_p, y_sorted, 0, 1, rows_c)
        xs = _combine_call(rt["pos"], gates, x1_s, mod_s, y_sorted, n_p, rows_c // ts, ts)

    st = lambda k: jnp.stack(outs[k])
    return (xp, xs, st("ckv_p"), st("kr_p"), st("conv_p"), st("gla_p"),
            st("ckv_s"), st("kr_s"), st("conv_s"), st("gla_s"))
```

```python
import functools

import numpy as np
import jax
import jax.numpy as jnp
from jax import lax
from jax.experimental import pallas as pl
from jax.experimental.pallas import tpu as pltpu

F32 = jnp.float32
BF16 = jnp.bfloat16

D_MODEL = 1024
CONV_DIM = 256
CONV_W = 3
MLA_HEADS = 8
MLA_NOPE = 64
MLA_ROPE = 32
MLA_V = 64
MLA_Q_LORA = 256
MLA_KV_LORA = 128
GLA_HEADS = 4
GLA_DK = 32
GLA_DV = 64
GLA_GATE_RANK = 16
GLA_NORMALIZER = 16.0
N_EXPERTS = 32
TOP_K = 4
D_FF = 1024
SWIGLU_ALPHA = 1.702
SWIGLU_LIMIT = 7.0
ROPE_THETA = 10000.0
EPS = 1e-6
N_MOD = 6
PAGE_SIZE = 128
IN_SIZES = (256, 256, 256, 256, 128, 32, 128, 128, 256, 16, 256)

LANES = 128
SUBLANES = 8

W_CONV_OUT = 3 * CONV_DIM
W_MLA_OUT = MLA_Q_LORA + MLA_KV_LORA + LANES
W_GLA_OUT = 128 + 128 + 256 + LANES + 256
W_IN_OUT = W_CONV_OUT + W_MLA_OUT + W_GLA_OUT
HEAD_W = LANES
QK_SCALE = (MLA_NOPE + MLA_ROPE) ** -0.5
LOG2_E = 1.4426950408889634
NEG_BIG = -1e30

MOE_TM = 256
COMBINE_TOK = 128
TOKEN_TILE = D_MODEL // LANES
DMA_UNROLL = 8


def _dot(a, b):
    return jnp.dot(a, b, preferred_element_type=F32)


def _dot_nt(a, b):
    return lax.dot_general(a, b, (((1,), (1,)), ((), ())), preferred_element_type=F32)


def _dot_tn(a, b):
    return lax.dot_general(a, b, (((0,), (0,)), ((), ())), preferred_element_type=F32)


def _sigmoid(x):
    return 1.0 / (1.0 + jnp.exp(-x))


def _params(sem, vmem_mb):
    return pltpu.CompilerParams(dimension_semantics=sem, vmem_limit_bytes=vmem_mb * 1024 * 1024)


def _mod_kernel(c_ref, w_ref, b_ref, o_ref):
    c = c_ref[...]
    a = (c * _sigmoid(c)).astype(BF16)
    o_ref[...] = _dot(a, w_ref[...].astype(BF16)) + b_ref[...]


def _mod_call(c_all, w_mod, b_mod):
    depth = w_mod.shape[0]
    nb = c_all.shape[0]
    return pl.pallas_call(
        _mod_kernel,
        grid=(depth, N_MOD),
        in_specs=[
            pl.BlockSpec((nb, D_MODEL), lambda l, j: (0, 0)),
            pl.BlockSpec((None, D_MODEL, D_MODEL), lambda l, j: (l, 0, j)),
            pl.BlockSpec((None, 1, D_MODEL), lambda l, j: (l, 0, j)),
        ],
        out_specs=pl.BlockSpec((None, nb, D_MODEL), lambda l, j: (l, 0, j)),
        out_shape=jax.ShapeDtypeStruct((depth, nb, N_MOD * D_MODEL), F32),
        compiler_params=_params(("arbitrary", "arbitrary"), 40),
        name="adaln_mod",
    )(c_all, w_mod, b_mod.reshape(depth, 1, N_MOD * D_MODEL))


def _in_kernel(x_ref, mod_ref, g_ref, w_ref, oc_ref, om_ref, og_ref):
    bb, tt, _ = x_ref.shape
    x = x_ref[...]
    ms = jnp.mean(x * x, axis=-1, keepdims=True)
    y = x * lax.rsqrt(ms + EPS) * g_ref[...]
    h = y * (1.0 + mod_ref[:, 1:2, :]) + mod_ref[:, 0:1, :]
    h2 = h.reshape(bb * tt, D_MODEL).astype(BF16)
    o = _dot(h2, w_ref[...])
    oc_ref[...] = o[:, 0:W_CONV_OUT].reshape(bb, tt, W_CONV_OUT)
    om_ref[...] = o[:, W_CONV_OUT:W_CONV_OUT + W_MLA_OUT].reshape(bb, tt, W_MLA_OUT)
    og_ref[...] = o[:, W_CONV_OUT + W_MLA_OUT:].reshape(bb, tt, W_GLA_OUT)


def _in_call(x, mod, g_mix, w_in_p, layer, bb, tt):
    b, t, _ = x.shape
    bs3 = lambda w: pl.BlockSpec((bb, tt, w), lambda i, j: (i, j, 0))
    return pl.pallas_call(
        _in_kernel,
        grid=(b // bb, t // tt),
        in_specs=[
            bs3(D_MODEL),
            pl.BlockSpec((bb, N_MOD, D_MODEL), lambda i, j: (i, 0, 0)),
            pl.BlockSpec((None, 1, D_MODEL), lambda i, j: (layer, 0, 0)),
            pl.BlockSpec((None, D_MODEL, W_IN_OUT), lambda i, j: (layer, 0, 0)),
        ],
        out_specs=[bs3(W_CONV_OUT), bs3(W_MLA_OUT), bs3(W_GLA_OUT)],
        out_shape=[jax.ShapeDtypeStruct((b, t, W_CONV_OUT), F32),
                   jax.ShapeDtypeStruct((b, t, W_MLA_OUT), F32),
                   jax.ShapeDtypeStruct((b, t, W_GLA_OUT), F32)],
        compiler_params=_params(("arbitrary", "arbitrary"), 48),
        name="in_proj",
    )(x, mod, g_mix, w_in_p)


def _segsum(x2, p_ref):
    return _dot(x2.astype(BF16), p_ref[...])


def _swap_rope_halves(x):
    lane = lax.broadcasted_iota(jnp.int32, x.shape, 1)
    return jnp.where(lane < 80, pltpu.roll(x, LANES - 16, 1), pltpu.roll(x, 16, 1))


def _qk_kernel(mla_ref, cos_ref, sin_ref, gqa_ref, wuq_ref, gq_ref, gkva_ref, wuk_ref, gk_ref, gkr_ref,
               p_ref, invn_ref, q_out, k_out, ckv_out, kr_out, *, q_scale):
    bb, tt, _ = mla_ref.shape
    rows = bb * tt
    mla = mla_ref[...].reshape(rows, W_MLA_OUT)
    cos = cos_ref[...]
    sin = sin_ref[...]
    invn = invn_ref[...]

    def head_norm_rope(xh, gain):
        rstd = lax.rsqrt(_segsum(xh * xh, p_ref) * invn + EPS)
        xn = xh * rstd * gain
        return xn * cos + _swap_rope_halves(xn) * sin

    q_lat = mla[:, 0:MLA_Q_LORA]
    qa = q_lat * lax.rsqrt(jnp.mean(q_lat * q_lat, axis=-1, keepdims=True) + EPS) * gqa_ref[...]
    q = _dot(qa.astype(BF16), wuq_ref[...])
    kv_lat = mla[:, MLA_Q_LORA:MLA_Q_LORA + MLA_KV_LORA]
    ckv = kv_lat * lax.rsqrt(jnp.mean(kv_lat * kv_lat, axis=-1, keepdims=True) + EPS) * gkva_ref[...]
    ckv_out[...] = ckv.reshape(bb, tt, MLA_KV_LORA)
    kr = head_norm_rope(mla[:, MLA_Q_LORA + MLA_KV_LORA:], gkr_ref[...])
    kr_out[...] = kr.reshape(bb, tt, LANES)
    k = _dot(ckv.astype(BF16), wuk_ref[...])
    for h in range(MLA_HEADS):
        qh = head_norm_rope(q[:, h * HEAD_W:(h + 1) * HEAD_W], gq_ref[...]) * q_scale
        q_out[:, h, :, :] = qh.astype(q_out.dtype).reshape(bb, tt, HEAD_W)
        kh = k[:, h * HEAD_W:(h + 1) * HEAD_W]
        rstd = lax.rsqrt(_segsum(kh * kh, p_ref) * invn + EPS)
        k_out[:, h, :, :] = (kh * rstd * gk_ref[...] + kr).astype(k_out.dtype).reshape(bb, tt, HEAD_W)


def _qk_call(mla, cos_t, sin_t, tab_index, wp, layer, bb, tt, qk_dtype, q_scale):
    b, t, _ = mla.shape
    rows = bb * tt
    vec = lambda arr: pl.BlockSpec((None, 1, arr.shape[-1]), lambda i, j: (layer, 0, 0))
    mat = lambda arr: pl.BlockSpec((None,) + arr.shape[1:], lambda i, j: (layer, 0, 0))
    tab = pl.BlockSpec((rows, LANES), lambda i, j: (tab_index(i, j), 0))
    const2 = lambda arr: pl.BlockSpec(arr.shape, lambda i, j: (0, 0))
    hk = pl.BlockSpec((bb, MLA_HEADS, tt, HEAD_W), lambda i, j: (i, 0, j, 0))
    return pl.pallas_call(
        functools.partial(_qk_kernel, q_scale=q_scale),
        grid=(b // bb, t // tt),
        in_specs=[pl.BlockSpec((bb, tt, W_MLA_OUT), lambda i, j: (i, j, 0)), tab, tab,
                  vec(wp["g_qa"]), mat(wp["w_uq"]), vec(wp["gq_lane"]), vec(wp["g_kva"]), mat(wp["w_uk"]),
                  vec(wp["gk_lane"]), vec(wp["gkr_lane"]), const2(wp["p_seg"]), const2(wp["invn"])],
        out_specs=[hk, hk,
                   pl.BlockSpec((bb, tt, MLA_KV_LORA), lambda i, j: (i, j, 0)),
                   pl.BlockSpec((bb, tt, LANES), lambda i, j: (i, j, 0))],
        out_shape=[jax.ShapeDtypeStruct((b, MLA_HEADS, t, HEAD_W), qk_dtype),
                   jax.ShapeDtypeStruct((b, MLA_HEADS, t, HEAD_W), qk_dtype),
                   jax.ShapeDtypeStruct((b, t, MLA_KV_LORA), F32),
                   jax.ShapeDtypeStruct((b, t, LANES), F32)],
        compiler_params=_params(("arbitrary", "arbitrary"), 40),
        name="mla_prep",
    )(mla, cos_t, sin_t, wp["g_qa"], wp["w_uq"], wp["gq_lane"], wp["g_kva"], wp["w_uk"], wp["gk_lane"],
      wp["gkr_lane"], wp["p_seg"], wp["invn"])


def _attn_prompt_kernel(q_ref, k_ref, vt_ref, wuvt_ref, o_ref, m_s, l_s, acc_s, *, tq):
    qi = pl.program_id(1)
    hq = MLA_HEADS * tq
    m_s[...] = jnp.full(m_s.shape, -jnp.inf, F32)
    l_s[...] = jnp.zeros(l_s.shape, F32)
    acc_s[...] = jnp.zeros(acc_s.shape, F32)

    def block(kb, masked):
        off = pl.multiple_of(kb * tq, tq)
        st = jnp.concatenate([_dot_nt(k_ref[h, pl.ds(off, tq), :], q_ref[h]) for h in range(MLA_HEADS)],
                             axis=1)
        if masked:
            key = lax.broadcasted_iota(jnp.int32, (tq, hq), 0)
            qry = lax.rem(lax.broadcasted_iota(jnp.int32, (tq, hq), 1), tq)
            st = jnp.where(key <= qry, st, -jnp.inf)
        m_old = m_s[...]
        m_new = jnp.maximum(m_old, jnp.max(st, axis=0, keepdims=True))
        p = jnp.exp2(st - m_new)
        alpha = jnp.exp2(m_old - m_new)
        m_s[...] = m_new
        l_s[...] = alpha * l_s[...] + jnp.sum(p, axis=0, keepdims=True)
        acc_s[...] = alpha * acc_s[...] + _dot(vt_ref[kb], p.astype(BF16))

    def body(kb, _):
        block(kb, False)
        return 0

    lax.fori_loop(0, qi, body, 0)
    block(qi, True)
    ot = (acc_s[...] / l_s[...]).astype(BF16)
    yt = jnp.zeros((MLA_HEADS * MLA_V, tq), F32)
    for h in range(MLA_HEADS):
        yt = yt + _dot(wuvt_ref[h], ot[:, h * tq:(h + 1) * tq])
    o_ref[...] = jnp.transpose(yt)


def _attn_prompt_call(q_cat, k_cat, ckv_t, w_uvpt, layer, tq):
    b, _, t, _ = q_cat.shape
    nkb = t // tq
    return pl.pallas_call(
        functools.partial(_attn_prompt_kernel, tq=tq),
        grid=(b, nkb),
        in_specs=[
            pl.BlockSpec((None, MLA_HEADS, tq, HEAD_W), lambda i, j: (i, 0, j, 0)),
            pl.BlockSpec((None, MLA_HEADS, t, HEAD_W), lambda i, j: (i, 0, 0, 0)),
            pl.BlockSpec((None, nkb, MLA_KV_LORA, tq), lambda i, j: (i, 0, 0, 0)),
            pl.BlockSpec((None, MLA_HEADS, MLA_HEADS * MLA_V, MLA_KV_LORA), lambda i, j: (layer, 0, 0, 0)),
        ],
        out_specs=pl.BlockSpec((None, tq, MLA_HEADS * MLA_V), lambda i, j: (i, j, 0)),
        out_shape=jax.ShapeDtypeStruct((b, t, MLA_HEADS * MLA_V), F32),
        scratch_shapes=[pltpu.VMEM((1, MLA_HEADS * tq), F32), pltpu.VMEM((1, MLA_HEADS * tq), F32),
                        pltpu.VMEM((MLA_KV_LORA, MLA_HEADS * tq), F32)],
        compiler_params=_params(("arbitrary", "arbitrary"), 48),
        name="attn_prompt",
    )(q_cat, k_cat, ckv_t, w_uvpt)


def _attn_sample_kernel(pt_ref, q_ref, ckvn_ref, krn_ref, wabs_ref, wukt_ref, wuv_ref,
                        cache_ckv, cache_krt, o_ref, ckv_buf, krt_buf, s_buf, sem,
                        *, layer, n_pages, chunk, ts):
    b = pl.program_id(0)
    nb = pl.num_programs(0)
    past = n_pages * PAGE_SIZE
    hq = MLA_HEADS * ts
    ppc = chunk // PAGE_SIZE
    n_chunks = past // chunk

    def page_copies(seq, slot, j):
        page = pt_ref[seq, j]
        ckv_dst = ckv_buf.at[slot, pl.ds(j * PAGE_SIZE, PAGE_SIZE), :]
        krt_dst = krt_buf.at[slot, j // ppc, :, pl.ds((j % ppc) * PAGE_SIZE, PAGE_SIZE)]
        return (pltpu.make_async_copy(cache_ckv.at[layer, page], ckv_dst, sem.at[slot, 0]),
                pltpu.make_async_copy(cache_krt.at[layer, page], krt_dst, sem.at[slot, 1]))

    def start_seq(seq, slot):
        for j in range(n_pages):
            c0, c1 = page_copies(seq, slot, j)
            c0.start()
            c1.start()

    def wait_seq(seq, slot):
        for j in range(n_pages):
            c0, c1 = page_copies(seq, slot, j)
            c0.wait()
            c1.wait()

    slot = lax.rem(b, 2)

    @pl.when(b == 0)
    def _():
        start_seq(0, 0)

    @pl.when(b + 1 < nb)
    def _():
        start_seq(b + 1, 1 - slot)

    wait_seq(b, slot)

    q_all = q_ref[...].reshape(hq, HEAD_W)
    q_wide = _dot(q_all.astype(BF16), wabs_ref[...])
    q_abs = jnp.concatenate(
        [q_wide[h * ts:(h + 1) * ts, h * MLA_KV_LORA:(h + 1) * MLA_KV_LORA] for h in range(MLA_HEADS)],
        axis=0).astype(BF16)
    q_rope = q_all[:, MLA_NOPE:MLA_NOPE + MLA_ROPE].astype(BF16)
    w_stack = jnp.concatenate([wukt_ref[...], q_abs], axis=0)
    nk = MLA_HEADS * MLA_NOPE

    def scores(c, rope_scores):
        res = _dot_nt(w_stack, c)
        k2 = res[0:nk, :] * res[0:nk, :]
        r_rows = []
        for h in range(MLA_HEADS):
            ssq = jnp.sum(k2[h * MLA_NOPE:(h + 1) * MLA_NOPE, :], axis=0, keepdims=True)
            r = lax.rsqrt(ssq * (1.0 / MLA_NOPE) + EPS)
            r_rows.append(jnp.broadcast_to(r, (ts, r.shape[1])))
        return res[nk:, :] * jnp.concatenate(r_rows, axis=0) + rope_scores

    def fold_lanes(x, op):
        out = x[:, 0:LANES]
        for i in range(1, x.shape[1] // LANES):
            out = op(out, x[:, i * LANES:(i + 1) * LANES])
        return out

    def pass1(ci, m_part):
        off = pl.multiple_of(ci * chunk, chunk)
        c = ckv_buf[slot, pl.ds(off, chunk), :].astype(BF16)
        s = scores(c, _dot(q_rope, krt_buf[slot, ci].astype(BF16)))
        s_buf[ci] = s
        return jnp.maximum(m_part, fold_lanes(s, jnp.maximum))

    m_part = lax.fori_loop(0, n_chunks, pass1, jnp.full((hq, LANES), -jnp.inf, F32))
    c_new = jnp.concatenate([ckvn_ref[...], jnp.zeros((LANES - ts, MLA_KV_LORA), F32)], axis=0).astype(BF16)
    kr_new = jnp.concatenate([krn_ref[...], jnp.zeros((LANES - ts, MLA_ROPE), F32)], axis=0).astype(BF16)
    s_new = scores(c_new, _dot_nt(q_rope, kr_new))
    jcol = lax.broadcasted_iota(jnp.int32, (hq, LANES), 1)
    trow = lax.rem(lax.broadcasted_iota(jnp.int32, (hq, LANES), 0), ts)
    s_new = jnp.where(jcol <= trow, s_new, -jnp.inf)
    m = jnp.maximum(jnp.max(m_part, axis=1, keepdims=True), jnp.max(s_new, axis=1, keepdims=True))

    def pass2(ci, carry):
        acc, l_part = carry
        off = pl.multiple_of(ci * chunk, chunk)
        p = jnp.exp(s_buf[ci] - m)
        acc = acc + _dot(p.astype(BF16), ckv_buf[slot, pl.ds(off, chunk), :].astype(BF16))
        return acc, l_part + fold_lanes(p, jnp.add)

    acc, l_part = lax.fori_loop(0, n_chunks, pass2,
                                (jnp.zeros((hq, MLA_KV_LORA), F32), jnp.zeros((hq, LANES), F32)))
    p_new = jnp.exp(s_new - m)
    acc = acc + _dot(p_new.astype(BF16), c_new)
    l = jnp.sum(l_part, axis=1, keepdims=True) + jnp.sum(p_new, axis=1, keepdims=True)
    o = (acc / l).astype(BF16)
    y = jnp.zeros((ts, MLA_HEADS * MLA_V), F32)
    for h in range(MLA_HEADS):
        y = y + _dot(o, wuv_ref[h])[h * ts:(h + 1) * ts, :]
    o_ref[...] = y


def _attn_sample_call(page_table, q_cat, ckv_new, kr_new, wp, cache_ckv, cache_krt, layer, chunk):
    b, _, ts, _ = q_cat.shape
    n_pages = page_table.shape[1]
    past = n_pages * PAGE_SIZE
    hq = MLA_HEADS * ts
    kern = functools.partial(_attn_sample_kernel, layer=layer, n_pages=n_pages, chunk=chunk, ts=ts)
    lay3 = lambda arr: pl.BlockSpec((None,) + arr.shape[1:], lambda i, pt: (layer,) + (0,) * (arr.ndim - 1))
    grid_spec = pltpu.PrefetchScalarGridSpec(
        num_scalar_prefetch=1,
        grid=(b,),
        in_specs=[
            pl.BlockSpec((None, MLA_HEADS, ts, HEAD_W), lambda i, pt: (i, 0, 0, 0)),
            pl.BlockSpec((None, ts, MLA_KV_LORA), lambda i, pt: (i, 0, 0)),
            pl.BlockSpec((None, ts, MLA_ROPE), lambda i, pt: (i, 0, 0)),
            lay3(wp["w_abs"]), lay3(wp["w_uk_t"]),
            lay3(wp["w_uvp"]),
            pl.BlockSpec(memory_space=pl.ANY),
            pl.BlockSpec(memory_space=pl.ANY),
        ],
        out_specs=pl.BlockSpec((None, ts, MLA_HEADS * MLA_V), lambda i, pt: (i, 0, 0)),
        scratch_shapes=[
            pltpu.VMEM((2, past, MLA_KV_LORA), F32),
            pltpu.VMEM((2, past // chunk, MLA_ROPE, chunk), F32),
            pltpu.VMEM((past // chunk, hq, chunk), F32),
            pltpu.SemaphoreType.DMA((2, 2)),
        ],
    )
    return pl.pallas_call(
        kern,
        grid_spec=grid_spec,
        out_shape=jax.ShapeDtypeStruct((b, ts, MLA_HEADS * MLA_V), F32),
        compiler_params=_params(("arbitrary",), 56),
        name="attn_sample",
    )(page_table, q_cat, ckv_new, kr_new, wp["w_abs"], wp["w_uk_t"], wp["w_uvp"], cache_ckv, cache_krt)


def _cumsum_rows(x):
    n = x.shape[0]
    row = lax.broadcasted_iota(jnp.int32, x.shape, 0)
    d = 1
    while d < n:
        x = x + jnp.where(row >= d, pltpu.roll(x, d, 0), 0.0)
        d *= 2
    return x


def _seq_kernel(conv_ref, gla_ref, cprev_ref, s0_ref, wconv_ref, wa2_ref, ba_ref, go_ref,
                segexp_ref, bdmask_ref, p256_ref,
                y_ref, cstate_ref, sstate_ref, s_scr, c_scr, v_scr):
    ci = pl.program_id(1)
    nci = pl.num_programs(1)
    ln = conv_ref.shape[0]
    kd = GLA_HEADS * GLA_DK
    vd = GLA_HEADS * GLA_DV

    @pl.when(ci == 0)
    def _():
        c_scr[...] = cprev_ref[...]
        s_scr[...] = jnp.zeros(s_scr.shape, F32)
        for h in range(GLA_HEADS):
            s_scr[h * GLA_DK:(h + 1) * GLA_DK, h * GLA_DV:(h + 1) * GLA_DV] = s0_ref[h]

    conv = conv_ref[...]
    v = conv[:, CONV_DIM:2 * CONV_DIM] * conv[:, 0:CONV_DIM]
    row = lax.broadcasted_iota(jnp.int32, (ln, CONV_DIM), 0)
    prev = c_scr[...]
    p0 = jnp.broadcast_to(prev[0:1, :], (ln, CONV_DIM))
    p1 = jnp.broadcast_to(prev[1:2, :], (ln, CONV_DIM))
    v1 = jnp.where(row == 0, p1, pltpu.roll(v, 1, 0))
    v2 = jnp.where(row == 0, p0, jnp.where(row == 1, p1, pltpu.roll(v, 2, 0)))
    wc = wconv_ref[...]
    y_a = conv[:, 2 * CONV_DIM:] * (v2 * wc[0:1, :] + v1 * wc[1:2, :] + v * wc[2:3, :])
    v_scr[...] = v
    c_scr[...] = v_scr[ln - 2:ln, :]
    y_ref[:, 0:CONV_DIM] = y_a

    gla = gla_ref[...]
    gq = gla[:, 0:kd] * (GLA_DK ** -0.5)
    gk = gla[:, kd:2 * kd]
    gv = gla[:, 2 * kd:2 * kd + vd]
    ga = gla[:, 2 * kd + vd:2 * kd + vd + LANES]
    gr = gla[:, 2 * kd + vd + LANES:]
    x = _dot(ga.astype(BF16), wa2_ref[...]) + ba_ref[...]
    log_a = (jnp.minimum(x, 0.0) - jnp.log(1.0 + jnp.exp(-jnp.abs(x)))) * (1.0 / GLA_NORMALIZER)
    cum = _cumsum_rows(log_a)
    s_prev = s_scr[...]
    o = _dot((gq * jnp.exp(cum)).astype(BF16), s_prev.astype(BF16))
    segexp = segexp_ref[...]
    rowk = lax.broadcasted_iota(jnp.int32, (ln, kd), 0)
    o_intra = [jnp.zeros((ln - g0, vd), F32) for g0 in range(0, ln, SUBLANES)]
    for s in range(ln):
        g = s // SUBLANES
        g0 = g * SUBLANES
        dec = jnp.exp(jnp.minimum(cum[g0:, :] - cum[s:s + 1, :], 0.0))
        dmat = jnp.where(rowk[g0:, :] >= s, gq[g0:, :] * gk[s:s + 1, :] * dec, 0.0)
        att = _dot(dmat.astype(BF16), segexp)
        o_intra[g] = o_intra[g] + att * gv[s:s + 1, :]
    intra = jnp.zeros((ln, vd), F32)
    for g, part in enumerate(o_intra):
        g0 = g * SUBLANES
        if g0 == 0:
            intra = intra + part
        else:
            intra = intra + jnp.concatenate([jnp.zeros((g0, vd), F32), part], axis=0)
    o = o + intra
    last = cum[ln - 1:ln, :]
    kdec = gk * jnp.exp(last - cum)
    e_col = jnp.transpose(jnp.broadcast_to(jnp.exp(last), (kd, kd)))
    e_col = jnp.concatenate([e_col, e_col], axis=1)
    s_new = e_col * s_prev + bdmask_ref[...] * _dot_tn(kdec.astype(BF16), gv.astype(BF16))
    s_scr[...] = s_new
    rstd = lax.rsqrt(_dot((o * o).astype(BF16), p256_ref[...]) * (1.0 / GLA_DV) + EPS)
    y_ref[:, CONV_DIM:] = o * rstd * go_ref[...] * (gr * _sigmoid(gr))

    @pl.when(ci == nci - 1)
    def _():
        cstate_ref[...] = c_scr[...]
        for h in range(GLA_HEADS):
            sstate_ref[h] = s_scr[h * GLA_DK:(h + 1) * GLA_DK, h * GLA_DV:(h + 1) * GLA_DV]


def _seq_call(conv, gla, conv_prev, s0, wp, layer, ln):
    b, t, _ = conv.shape
    kd = GLA_HEADS * GLA_DK
    vd = GLA_HEADS * GLA_DV
    lay = lambda arr: pl.BlockSpec((None,) + arr.shape[1:], lambda i, j: (layer,) + (0,) * (arr.ndim - 1))
    const2 = lambda arr: pl.BlockSpec(arr.shape, lambda i, j: (0, 0))
    return pl.pallas_call(
        _seq_kernel,
        grid=(b, t // ln),
        in_specs=[
            pl.BlockSpec((None, ln, W_CONV_OUT), lambda i, j: (i, j, 0)),
            pl.BlockSpec((None, ln, W_GLA_OUT), lambda i, j: (i, j, 0)),
            pl.BlockSpec((None, CONV_W - 1, CONV_DIM), lambda i, j: (i, 0, 0)),
            pl.BlockSpec((None, GLA_HEADS, GLA_DK, GLA_DV), lambda i, j: (i, 0, 0, 0)),
            lay(wp["w_conv"]), lay(wp["w_a2p"]), lay(wp["b_a"]), lay(wp["g_gla_lane"]),
            const2(wp["seg_exp"]), const2(wp["bd_mask"]), const2(wp["p256"]),
        ],
        out_specs=[
            pl.BlockSpec((None, ln, CONV_DIM + vd), lambda i, j: (i, j, 0)),
            pl.BlockSpec((None, CONV_W - 1, CONV_DIM), lambda i, j: (i, 0, 0)),
            pl.BlockSpec((None, GLA_HEADS, GLA_DK, GLA_DV), lambda i, j: (i, 0, 0, 0)),
        ],
        out_shape=[jax.ShapeDtypeStruct((b, t, CONV_DIM + vd), F32),
                   jax.ShapeDtypeStruct((b, CONV_W - 1, CONV_DIM), F32),
                   jax.ShapeDtypeStruct((b, GLA_HEADS, GLA_DK, GLA_DV), F32)],
        scratch_shapes=[pltpu.VMEM((kd, vd), F32), pltpu.VMEM((CONV_W - 1, CONV_DIM), F32),
                        pltpu.VMEM((ln, CONV_DIM), F32)],
        compiler_params=_params(("arbitrary", "arbitrary"), 40),
        name="conv_gla",
    )(conv, gla, conv_prev, s0, wp["w_conv"], wp["w_a2p"], wp["b_a"], wp["g_gla_lane"],
      wp["seg_exp"], wp["bd_mask"], wp["p256"])


def _store_token_tiles(ref, x):
    rows = x.shape[0]
    for s in range(TOKEN_TILE):
        ref[pl.ds(s, rows, stride=TOKEN_TILE), :] = x[:, s * LANES:(s + 1) * LANES]


def _load_token_tiles(ref, base, rows):
    return jnp.concatenate(
        [ref[pl.ds(base + s, rows, stride=TOKEN_TILE), :] for s in range(TOKEN_TILE)], axis=1)


def _split_bf16(x):
    hi = x.astype(BF16)
    lo = (x - hi.astype(F32)).astype(BF16)
    return hi, lo


def _out_kernel(yac_ref, yb_ref, x_ref, mod_ref, woac_ref, wob_ref, g_ref, wrh_ref, wrl_ref, br_ref,
                x1_ref, h_ref, idx_ref, gate_ref):
    bb, tt, _ = x_ref.shape
    rows = bb * tt
    yac = yac_ref[...].reshape(rows, 2 * CONV_DIM).astype(BF16)
    yb = yb_ref[...].reshape(rows, MLA_HEADS * MLA_V).astype(BF16)
    y = (_dot(yac, woac_ref[...]) + _dot(yb, wob_ref[...])).reshape(bb, tt, D_MODEL)
    x1 = x_ref[...] + mod_ref[:, 2:3, :] * y
    x1_ref[...] = x1
    ms = jnp.mean(x1 * x1, axis=-1, keepdims=True)
    h = x1 * lax.rsqrt(ms + EPS) * g_ref[...]
    h = (h * (1.0 + mod_ref[:, 4:5, :]) + mod_ref[:, 3:4, :]).reshape(rows, D_MODEL)
    _store_token_tiles(h_ref, h)
    h_hi, h_lo = _split_bf16(h)
    logits = _dot(h_hi, wrh_ref[...]) + (_dot(h_hi, wrl_ref[...]) + _dot(h_lo, wrh_ref[...])) + br_ref[...]
    lane = lax.broadcasted_iota(jnp.int32, (rows, LANES), 1)
    lane_f = lane.astype(F32)
    idx_out = jnp.zeros((rows, LANES), F32)
    val_out = jnp.zeros((rows, LANES), F32)
    vals = []
    for k in range(TOP_K):
        mx = jnp.max(logits, axis=-1, keepdims=True)
        ix = jnp.min(jnp.where(logits == mx, lane_f, float(LANES)), axis=-1, keepdims=True)
        idx_out = jnp.where(lane == k, ix, idx_out)
        vals.append(mx)
        logits = jnp.where(lane_f == ix, -jnp.inf, logits)
    es = [jnp.exp(v - vals[0]) for v in vals]
    den = es[0] + es[1] + es[2] + es[3]
    for k in range(TOP_K):
        val_out = jnp.where(lane == k, es[k] / den, val_out)
    idx_ref[...] = idx_out.astype(jnp.int32)
    gate_ref[...] = val_out


def _out_call(yac, yb, x, mod, wp, layer, bb, tt):
    b, t, _ = x.shape
    n = b * t
    rows = bb * tt
    nj = t // tt
    bs3 = lambda w: pl.BlockSpec((bb, tt, w), lambda i, j: (i, j, 0))
    lay = lambda arr: pl.BlockSpec((None,) + arr.shape[1:], lambda i, j: (layer,) + (0,) * (arr.ndim - 1))
    flat = lambda w: pl.BlockSpec((rows, w), lambda i, j: (i * nj + j, 0))
    return pl.pallas_call(
        _out_kernel,
        grid=(b // bb, nj),
        in_specs=[bs3(2 * CONV_DIM), bs3(MLA_HEADS * MLA_V), bs3(D_MODEL),
                  pl.BlockSpec((bb, N_MOD, D_MODEL), lambda i, j: (i, 0, 0)),
                  lay(wp["w_o_ac"]), lay(wp["w_o_b"]), lay(wp["g_ffn"]),
                  lay(wp["w_r_hi"]), lay(wp["w_r_lo"]), lay(wp["b_r"])],
        out_specs=[bs3(D_MODEL), pl.BlockSpec((rows * TOKEN_TILE, LANES), lambda i, j: (i * nj + j, 0)),
                   flat(LANES), flat(LANES)],
        out_shape=[jax.ShapeDtypeStruct((b, t, D_MODEL), F32),
                   jax.ShapeDtypeStruct((n * TOKEN_TILE, LANES), F32),
                   jax.ShapeDtypeStruct((n, LANES), jnp.int32),
                   jax.ShapeDtypeStruct((n, LANES), F32)],
        compiler_params=_params(("arbitrary", "arbitrary"), 48),
        name="out_proj_router",
    )(yac, yb, x, mod, wp["w_o_ac"], wp["w_o_b"], wp["g_ffn"], wp["w_r_hi"], wp["w_r_lo"], wp["b_r"])


def _row_dma_loop(n_rows, make_copy, start):
    def body(g, _):
        for u in range(DMA_UNROLL):
            cp = make_copy(g * DMA_UNROLL + u)
            if start:
                cp.start()
            else:
                cp.wait()
        return 0
    lax.fori_loop(0, n_rows // DMA_UNROLL, body, 0)


def _experts_kernel(stok_ref, be_ref, src0_ref, nrows_ref, nv_ref, h_hbm, h2d_hbm, wg_ref, bg_ref, wu_ref, bu_ref,
                    wd_ref, bd_ref, o_ref, xbuf, wg_s, wu_s, wd_s, sem):
    i = pl.program_id(0)
    nv = nv_ref[0]

    def start_gather(blk, slot):
        first = src0_ref[blk]
        last = first + nrows_ref[blk] - 1

        def make_copy(r):
            tok = stok_ref[jnp.minimum(first + r, last)]
            dst = xbuf.at[slot, pl.ds(pl.multiple_of(r * TOKEN_TILE, TOKEN_TILE), TOKEN_TILE), :]
            return pltpu.make_async_copy(h_hbm.at[tok], dst, sem.at[slot])

        _row_dma_loop(MOE_TM, make_copy, True)

    def wait_gather(slot):
        pltpu.make_async_copy(h2d_hbm.at[pl.ds(0, MOE_TM * TOKEN_TILE)], xbuf.at[slot], sem.at[slot]).wait()

    slot = lax.rem(i, 2)

    @pl.when(i == 0)
    def _():
        start_gather(0, 0)

    @pl.when(i + 1 < nv)
    def _():
        start_gather(i + 1, 1 - slot)

    @pl.when(i >= nv)
    def _():
        o_ref[...] = jnp.zeros(o_ref.shape, F32)

    @pl.when(i < nv)
    def _():
        wait_gather(slot)
        prev = be_ref[jnp.maximum(i - 1, 0)]

        @pl.when((i == 0) | (be_ref[i] != prev))
        def _():
            wg_s[...] = wg_ref[...].astype(BF16)
            wu_s[...] = wu_ref[...].astype(BF16)
            wd_s[...] = wd_ref[...].astype(BF16)

        x = _load_token_tiles(xbuf.at[slot], 0, MOE_TM).astype(BF16)
        g = _dot(x, wg_s[...]) + bg_ref[...]
        u = _dot(x, wu_s[...]) + bu_ref[...]
        g = jnp.minimum(g, SWIGLU_LIMIT)
        u = jnp.clip(u, -SWIGLU_LIMIT, SWIGLU_LIMIT)
        a = (u + 1.0) * (g * _sigmoid(SWIGLU_ALPHA * g))
        _store_token_tiles(o_ref, _dot(a.astype(BF16), wd_s[...]) + bd_ref[...])


def _experts_call(rt, h_tiles, w_gate, b_gate, w_up, b_up, w_down, b_down, layer, p_rows):
    nblk = p_rows // MOE_TM
    eff = lambda i, nv: jnp.minimum(i, nv[0] - 1)
    wspec = pl.BlockSpec((None, None, D_MODEL, D_FF),
                         lambda i, st, be, s0, nr, nv: (layer, be[eff(i, nv)], 0, 0))
    bspec = pl.BlockSpec((None, None, 1, D_FF), lambda i, st, be, s0, nr, nv: (layer, be[eff(i, nv)], 0, 0))
    grid_spec = pltpu.PrefetchScalarGridSpec(
        num_scalar_prefetch=5,
        grid=(nblk,),
        in_specs=[pl.BlockSpec(memory_space=pl.ANY), pl.BlockSpec(memory_space=pl.ANY),
                  wspec, bspec, wspec, bspec, wspec, bspec],
        out_specs=pl.BlockSpec((MOE_TM * TOKEN_TILE, LANES), lambda i, st, be, s0, nr, nv: (i, 0)),
        scratch_shapes=[pltpu.VMEM((2, MOE_TM * TOKEN_TILE, LANES), F32),
                        pltpu.VMEM((D_MODEL, D_FF), BF16), pltpu.VMEM((D_MODEL, D_FF), BF16),
                        pltpu.VMEM((D_FF, D_MODEL), BF16), pltpu.SemaphoreType.DMA((2,))],
    )
    depth = w_gate.shape[0]
    b4 = lambda bias: bias.reshape(depth, N_EXPERTS, 1, bias.shape[-1])
    return pl.pallas_call(
        _experts_kernel,
        grid_spec=grid_spec,
        out_shape=jax.ShapeDtypeStruct((p_rows * TOKEN_TILE, LANES), F32),
        compiler_params=_params(("arbitrary",), 56),
        name="moe_experts",
    )(rt["stok"], rt["blk_e"], rt["src0"], rt["nrows"], rt["n_valid"],
      h_tiles.reshape(-1, TOKEN_TILE, LANES), h_tiles.reshape(-1, LANES),
      w_gate, b4(b_gate), w_up, b4(b_up), w_down, b4(b_down))


def _combine_kernel(pos_ref, gate_ref, x1_ref, mod_ref, yb_hbm, yb2d_hbm, o_ref, buf, sem):
    bb, tt, _ = x1_ref.shape
    rows = bb * tt

    def make_copy(r):
        dst = buf.at[pl.ds(pl.multiple_of(r * TOKEN_TILE, TOKEN_TILE), TOKEN_TILE), :]
        return pltpu.make_async_copy(yb_hbm.at[pos_ref[0, 0, r]], dst, sem.at[0])

    _row_dma_loop(rows * TOP_K, make_copy, True)
    pltpu.make_async_copy(yb2d_hbm.at[pl.ds(0, rows * TOP_K * TOKEN_TILE)], buf, sem.at[0]).wait()
    gates = gate_ref[...]
    y = _load_token_tiles(buf, 0, rows) * gates[:, 0:1]
    for k in range(1, TOP_K):
        y = y + _load_token_tiles(buf, k * rows * TOKEN_TILE, rows) * gates[:, k:k + 1]
    o_ref[...] = x1_ref[...] + mod_ref[:, 5:6, :] * y.reshape(bb, tt, D_MODEL)


def _combine_call(pos, gates, x1, mod, yb, tok0, bb, tt):
    b, t, _ = x1.shape
    rows = bb * tt
    nj = t // tt
    blk0 = tok0 // rows
    n_all = pos.shape[0] // TOP_K
    pos3 = jnp.swapaxes(pos.reshape(n_all // rows, rows, TOP_K), 1, 2).reshape(n_all // rows, 1, rows * TOP_K)
    return pl.pallas_call(
        _combine_kernel,
        grid=(b // bb, nj),
        in_specs=[
            pl.BlockSpec((1, 1, rows * TOP_K), lambda i, j: (blk0 + i * nj + j, 0, 0), memory_space=pltpu.SMEM),
            pl.BlockSpec((rows, LANES), lambda i, j: (blk0 + i * nj + j, 0)),
            pl.BlockSpec((bb, tt, D_MODEL), lambda i, j: (i, j, 0)),
            pl.BlockSpec((bb, N_MOD, D_MODEL), lambda i, j: (i, 0, 0)),
            pl.BlockSpec(memory_space=pl.ANY),
            pl.BlockSpec(memory_space=pl.ANY),
        ],
        out_specs=pl.BlockSpec((bb, tt, D_MODEL), lambda i, j: (i, j, 0)),
        out_shape=jax.ShapeDtypeStruct((b, t, D_MODEL), F32),
        scratch_shapes=[pltpu.VMEM((TOP_K * rows * TOKEN_TILE, LANES), F32), pltpu.SemaphoreType.DMA((1,))],
        compiler_params=_params(("arbitrary", "arbitrary"), 32),
        name="moe_combine",
    )(pos3, gates, x1, mod, yb.reshape(-1, TOKEN_TILE, LANES), yb.reshape(-1, LANES))


def _route(top_i, n_tokens):
    m = n_tokens * TOP_K
    flat_e = top_i.reshape(m)
    onehot = (flat_e[:, None] == jnp.arange(N_EXPERTS, dtype=jnp.int32)[None, :]).astype(jnp.int32)
    csum = jnp.cumsum(onehot, axis=0)
    rank = jnp.sum(onehot * csum, axis=1) - 1
    counts = csum[-1]
    padded = (counts + MOE_TM - 1) // MOE_TM * MOE_TM
    pends = jnp.cumsum(padded)
    pstarts = pends - padded
    pos = jnp.sum(onehot * pstarts[None, :], axis=1) + rank
    p_rows = (m // MOE_TM + N_EXPERTS) * MOE_TM
    nblk = p_rows // MOE_TM
    blk_start = jnp.arange(nblk, dtype=jnp.int32) * MOE_TM
    blk_e = jnp.minimum(jnp.sum((blk_start[:, None] >= pends[None, :]).astype(jnp.int32), axis=1), N_EXPERTS - 1)
    n_valid = (pends[-1] // MOE_TM).reshape(1)
    slot = jnp.arange(m, dtype=jnp.int32)
    stok = (jnp.sort(flat_e * m + slot) % m) // TOP_K
    starts = jnp.cumsum(counts) - counts
    off = blk_start - pstarts[blk_e]
    src0 = jnp.clip(starts[blk_e] + off, 0, m - 1)
    nrows = jnp.clip(counts[blk_e] - off, 1, MOE_TM)
    route = dict(pos=pos, blk_e=blk_e, n_valid=n_valid, stok=stok, src0=src0, nrows=nrows)
    return {k: v.astype(jnp.int32) for k, v in route.items()}, p_rows


def _prep_weights(g_mix, g_ffn, w_in, w_o, w_conv, g_qa, w_uq, g_qn, g_qr, g_kva, g_kr, w_uk, g_kn, w_uv,
                  w_a2, b_a, g_gla_o, w_router, b_router):
    depth = w_in.shape[0]
    offs = np.cumsum((0,) + IN_SIZES)
    seg = lambda i: w_in[:, :, offs[i]:offs[i + 1]]
    z = lambda n: jnp.zeros((depth, D_MODEL, n), w_in.dtype)
    w_in_p = jnp.concatenate(
        [seg(0), seg(1), seg(2), seg(3), seg(4), z(64), seg(5), z(32),
         seg(6), seg(7), seg(8), seg(9), z(LANES - GLA_GATE_RANK), seg(10)], axis=-1).astype(BF16)
    pad_last = lambda a, n: jnp.pad(a, [(0, 0)] * (a.ndim - 1) + [(0, n - a.shape[-1])])
    v3 = lambda a: a.reshape(depth, 1, a.shape[-1])
    wp = {}
    wp["g_mix"] = v3(g_mix)
    wp["g_ffn"] = v3(g_ffn)
    wp["w_in_p"] = w_in_p
    wp["g_qa"] = v3(g_qa)
    wp["w_uq"] = pad_last(w_uq, HEAD_W).reshape(depth, MLA_Q_LORA, MLA_HEADS * HEAD_W).astype(BF16)
    wp["gq_lane"] = v3(pad_last(jnp.concatenate([g_qn, g_qr], axis=-1), HEAD_W))
    wp["g_kva"] = v3(g_kva)
    wp["w_uk"] = pad_last(w_uk, HEAD_W).reshape(depth, MLA_KV_LORA, MLA_HEADS * HEAD_W).astype(BF16)
    wp["w_uk_t"] = jnp.swapaxes(w_uk.reshape(depth, MLA_KV_LORA, MLA_HEADS * MLA_NOPE), 1, 2).astype(BF16)
    wp["gk_lane"] = v3(pad_last(g_kn, HEAD_W))
    wp["gkr_lane"] = v3(pad_last(jnp.concatenate([jnp.zeros((depth, MLA_NOPE), F32), g_kr], axis=-1), HEAD_W))
    w_abs = jnp.transpose(w_uk, (0, 3, 2, 1)) * g_kn[:, :, None, None]
    w_abs = w_abs.reshape(depth, MLA_NOPE, MLA_HEADS * MLA_KV_LORA)
    wp["w_abs"] = jnp.pad(w_abs, ((0, 0), (0, HEAD_W - MLA_NOPE), (0, 0))).astype(BF16)
    eye_h = jnp.eye(MLA_HEADS, dtype=F32)
    wp["w_uvp"] = jnp.einsum("lrhv,hg->lhrgv", w_uv, eye_h).reshape(
        depth, MLA_HEADS, MLA_KV_LORA, MLA_HEADS * MLA_V).astype(BF16)
    wp["w_uvpt"] = jnp.swapaxes(wp["w_uvp"], 2, 3)
    wp["w_o_ac"] = jnp.concatenate([w_o[:, 0:CONV_DIM], w_o[:, CONV_DIM + MLA_HEADS * MLA_V:]], axis=1).astype(BF16)
    wp["w_o_b"] = w_o[:, CONV_DIM:CONV_DIM + MLA_HEADS * MLA_V].astype(BF16)
    wp["w_conv"] = w_conv
    wp["w_a2p"] = jnp.pad(w_a2, ((0, 0), (0, LANES - GLA_GATE_RANK), (0, 0))).astype(BF16)
    wp["b_a"] = v3(b_a)
    wp["g_gla_lane"] = v3(jnp.tile(g_gla_o, (1, GLA_HEADS)))
    w_r = pad_last(w_router, LANES)
    w_r_hi = w_r.astype(BF16)
    wp["w_r_hi"] = w_r_hi
    wp["w_r_lo"] = (w_r - w_r_hi.astype(F32)).astype(BF16)
    wp["b_r"] = v3(jnp.concatenate([b_router, jnp.full((depth, LANES - N_EXPERTS), NEG_BIG, F32)], axis=-1))
    lane = np.arange(LANES)
    seg_id = np.where(lane < MLA_NOPE, 0, np.where(lane < MLA_NOPE + MLA_ROPE, 1, 2 + lane))
    wp["p_seg"] = jnp.asarray((seg_id[:, None] == seg_id[None, :]).astype(np.float32), BF16)
    wp["invn"] = jnp.asarray(np.where(lane < MLA_NOPE, 1.0 / MLA_NOPE, 1.0 / MLA_ROPE).astype(np.float32))[None, :]
    kd = GLA_HEADS * GLA_DK
    vd = GLA_HEADS * GLA_DV
    hk = np.arange(kd) // GLA_DK
    hv = np.arange(vd) // GLA_DV
    bd = (hk[:, None] == hv[None, :]).astype(np.float32)
    wp["seg_exp"] = jnp.asarray(bd, BF16)
    wp["bd_mask"] = jnp.asarray(bd, F32)
    wp["p256"] = jnp.asarray((hv[:, None] == hv[None, :]).astype(np.float32), BF16)
    return wp


def _rope_tables(pos):
    half = MLA_ROPE // 2
    freqs = ROPE_THETA ** (-jnp.arange(half, dtype=F32) / half)
    ang = pos.astype(F32)[:, None] * freqs
    cos, sin = jnp.cos(ang), jnp.sin(ang)
    t = pos.shape[0]
    cos_t = jnp.concatenate([jnp.ones((t, MLA_NOPE), F32), cos, cos, jnp.zeros((t, HEAD_W - 96), F32)], axis=-1)
    sin_t = jnp.concatenate([jnp.zeros((t, MLA_NOPE), F32), -sin, sin, jnp.zeros((t, HEAD_W - 96), F32)], axis=-1)
    return cos_t, sin_t


def kernel(x_prompt, x_sample, cache_ckv, cache_kr, state_conv, state_gla, page_table, c_prompt, c_sample, g_mix, g_ffn, w_mod, b_mod, w_in, w_o, w_conv, g_qa, w_uq, g_qn, g_qr, g_kva, g_kr, w_uk, g_kn, w_uv, w_a2, b_a, g_gla_o, w_router, b_router, w_gate, b_gate, w_up, b_up, w_down, b_down):
    depth = w_in.shape[0]
    bp, tp, _ = x_prompt.shape
    bs, ts, _ = x_sample.shape
    n_pages = page_table.shape[1]
    past = n_pages * PAGE_SIZE
    n_p, n_s = bp * tp, bs * ts
    n_all = n_p + n_s

    tt_p = min(512, tp)
    bb_s = min(16, bs)
    tq = min(256, tp)
    ln_p = min(64, tp)
    chunk_s = min(1024, max(PAGE_SIZE, past // 2))
    rows_c = COMBINE_TOK

    wp = _prep_weights(g_mix, g_ffn, w_in, w_o, w_conv, g_qa, w_uq, g_qn, g_qr, g_kva, g_kr, w_uk, g_kn, w_uv,
                       w_a2, b_a, g_gla_o, w_router, b_router)
    cache_krt = jnp.swapaxes(cache_kr, 2, 3)

    nb_mod = -(-(bp + bs) // 16) * 16
    c_all = jnp.concatenate([c_prompt, c_sample, jnp.zeros((nb_mod - bp - bs, D_MODEL), F32)], axis=0)
    mod_all = _mod_call(c_all, w_mod, b_mod)

    cos_p, sin_p = _rope_tables(jnp.arange(tp, dtype=jnp.int32))
    cos_s, sin_s = _rope_tables(past + jnp.arange(ts, dtype=jnp.int32))
    cos_s, sin_s = jnp.tile(cos_s, (bb_s, 1)), jnp.tile(sin_s, (bb_s, 1))

    conv0 = jnp.zeros((bp, CONV_W - 1, CONV_DIM), F32)
    gla0 = jnp.zeros((bp, GLA_HEADS, GLA_DK, GLA_DV), F32)

    xp, xs = x_prompt, x_sample
    outs = {k: [] for k in ("ckv_p", "kr_p", "conv_p", "gla_p", "ckv_s", "kr_s", "conv_s", "gla_s")}
    for l in range(depth):
        mod_p = mod_all[l, 0:bp].reshape(bp, N_MOD, D_MODEL)
        mod_s = mod_all[l, bp:bp + bs].reshape(bs, N_MOD, D_MODEL)

        conv_o, mla_o, gla_o = _in_call(xp, mod_p, wp["g_mix"], wp["w_in_p"], l, 1, tt_p)
        q_cat, k_cat, ckv, kr128 = _qk_call(mla_o, cos_p, sin_p, lambda i, j: j, wp, l, 1, tt_p, BF16,
                                            QK_SCALE * LOG2_E)
        ckv_t = jnp.swapaxes(ckv.astype(BF16).reshape(bp, tp // tq, tq, MLA_KV_LORA), 2, 3)
        yb_p = _attn_prompt_call(q_cat, k_cat, ckv_t, wp["w_uvpt"], l, tq)
        yac_p, conv_st, gla_st = _seq_call(conv_o, gla_o, conv0, gla0, wp, l, ln_p)
        outs["ckv_p"].append(ckv)
        outs["kr_p"].append(kr128[:, :, MLA_NOPE:MLA_NOPE + MLA_ROPE])
        outs["conv_p"].append(conv_st)
        outs["gla_p"].append(gla_st)
        x1_p, h_p, idx_p, gate_p = _out_call(yac_p, yb_p, xp, mod_p, wp, l, 1, tt_p)

        conv_o, mla_o, gla_o = _in_call(xs, mod_s, wp["g_mix"], wp["w_in_p"], l, bb_s, ts)
        q_cat, _, ckv, kr128 = _qk_call(mla_o, cos_s, sin_s, lambda i, j: 0, wp, l, bb_s, ts, F32, QK_SCALE)
        kr_s = kr128[:, :, MLA_NOPE:MLA_NOPE + MLA_ROPE]
        yb_s = _attn_sample_call(page_table, q_cat, ckv, kr_s, wp, cache_ckv, cache_krt, l, chunk_s)
        yac_s, conv_st, gla_st = _seq_call(conv_o, gla_o, state_conv[l], state_gla[l], wp, l, ts)
        outs["ckv_s"].append(ckv)
        outs["kr_s"].append(kr_s)
        outs["conv_s"].append(conv_st)
        outs["gla_s"].append(gla_st)
        x1_s, h_s, idx_s, gate_s = _out_call(yac_s, yb_s, xs, mod_s, wp, l, bb_s, ts)

        h_all = jnp.concatenate([h_p, h_s], axis=0)
        top_i = jnp.concatenate([idx_p[:, 0:TOP_K], idx_s[:, 0:TOP_K]], axis=0)
        gates = jnp.concatenate([gate_p, gate_s], axis=0)
        rt, p_rows = _route(top_i, n_all)
        h_tiles = h_all.reshape(n_all, TOKEN_TILE, LANES)
        y_sorted = _experts_call(rt, h_tiles, w_gate, b_gate, w_up, b_up, w_down, b_down, l, p_rows)
        y_sorted = y_sorted.reshape(p_rows, TOKEN_TILE, LANES)
        xp = _combine_call(rt["pos"], gates, x1_p, mod_p, y_sorted, 0, 1, rows_c)
        xs = _combine_call(rt["pos"], gates, x1_s, mod_s, y_sorted, n_p, rows_c // ts, ts)

    st = lambda k: jnp.stack(outs[k])
    return (xp, xs, st("ckv_p"), st("kr_p"), st("conv_p"), st("gla_p"),
            st("ckv_s"), st("kr_s"), st("conv_s"), st("gla_s"))
```

```python
import functools

import numpy as np
import jax
import jax.numpy as jnp
from jax import lax
from jax.experimental import pallas as pl
from jax.experimental.pallas import tpu as pltpu

F32 = jnp.float32
BF16 = jnp.bfloat16

D_MODEL = 1024
CONV_DIM = 256
CONV_W = 3
MLA_HEADS = 8
MLA_NOPE = 64
MLA_ROPE = 32
MLA_V = 64
MLA_Q_LORA = 256
MLA_KV_LORA = 128
GLA_HEADS = 4
GLA_DK = 32
GLA_DV = 64
GLA_GATE_RANK = 16
GLA_NORMALIZER = 16.0
N_EXPERTS = 32
TOP_K = 4
D_FF = 1024
SWIGLU_ALPHA = 1.702
SWIGLU_LIMIT = 7.0
ROPE_THETA = 10000.0
EPS = 1e-6
N_MOD = 6
PAGE_SIZE = 128
IN_SIZES = (256, 256, 256, 256, 128, 32, 128, 128, 256, 16, 256)

LANES = 128
SUBLANES = 8

W_CONV_OUT = 3 * CONV_DIM
W_MLA_OUT = MLA_Q_LORA + MLA_KV_LORA + LANES
W_GLA_OUT = 128 + 128 + 256 + LANES + 256
W_IN_OUT = W_CONV_OUT + W_MLA_OUT + W_GLA_OUT
HEAD_W = LANES
QK_SCALE = (MLA_NOPE + MLA_ROPE) ** -0.5
LOG2_E = 1.4426950408889634
NEG_BIG = -1e30

MOE_TM = 256
COMBINE_TOK = 128
TOKEN_TILE = D_MODEL // LANES
DMA_UNROLL = 8


def _dot(a, b):
    return jnp.dot(a, b, preferred_element_type=F32)


def _dot_nt(a, b):
    return lax.dot_general(a, b, (((1,), (1,)), ((), ())), preferred_element_type=F32)


def _dot_tn(a, b):
    return lax.dot_general(a, b, (((0,), (0,)), ((), ())), preferred_element_type=F32)


def _sigmoid(x):
    return 1.0 / (1.0 + jnp.exp(-x))


def _params(sem, vmem_mb):
    return pltpu.CompilerParams(dimension_semantics=sem, vmem_limit_bytes=vmem_mb * 1024 * 1024)


def _mod_kernel(c_ref, w_ref, b_ref, o_ref):
    c = c_ref[...]
    a = (c * _sigmoid(c)).astype(BF16)
    o_ref[...] = _dot(a, w_ref[...].astype(BF16)) + b_ref[...]


def _mod_call(c_all, w_mod, b_mod):
    depth = w_mod.shape[0]
    nb = c_all.shape[0]
    return pl.pallas_call(
        _mod_kernel,
        grid=(depth, N_MOD),
        in_specs=[
            pl.BlockSpec((nb, D_MODEL), lambda l, j: (0, 0)),
            pl.BlockSpec((None, D_MODEL, D_MODEL), lambda l, j: (l, 0, j)),
            pl.BlockSpec((None, 1, D_MODEL), lambda l, j: (l, 0, j)),
        ],
        out_specs=pl.BlockSpec((None, nb, D_MODEL), lambda l, j: (l, 0, j)),
        out_shape=jax.ShapeDtypeStruct((depth, nb, N_MOD * D_MODEL), F32),
        compiler_params=_params(("arbitrary", "arbitrary"), 40),
        name="adaln_mod",
    )(c_all, w_mod, b_mod.reshape(depth, 1, N_MOD * D_MODEL))


def _in_kernel(x_ref, mod_ref, g_ref, w_ref, oc_ref, om_ref, og_ref):
    bb, tt, _ = x_ref.shape
    x = x_ref[...]
    ms = jnp.mean(x * x, axis=-1, keepdims=True)
    y = x * lax.rsqrt(ms + EPS) * g_ref[...]
    h = y * (1.0 + mod_ref[:, 1:2, :]) + mod_ref[:, 0:1, :]
    h2 = h.reshape(bb * tt, D_MODEL).astype(BF16)
    o = _dot(h2, w_ref[...])
    oc_ref[...] = o[:, 0:W_CONV_OUT].reshape(bb, tt, W_CONV_OUT)
    om_ref[...] = o[:, W_CONV_OUT:W_CONV_OUT + W_MLA_OUT].reshape(bb, tt, W_MLA_OUT)
    og_ref[...] = o[:, W_CONV_OUT + W_MLA_OUT:].reshape(bb, tt, W_GLA_OUT)


def _in_call(x, mod, g_mix, w_in_p, layer, bb, tt):
    b, t, _ = x.shape
    bs3 = lambda w: pl.BlockSpec((bb, tt, w), lambda i, j: (i, j, 0))
    return pl.pallas_call(
        _in_kernel,
        grid=(b // bb, t // tt),
        in_specs=[
            bs3(D_MODEL),
            pl.BlockSpec((bb, N_MOD, D_MODEL), lambda i, j: (i, 0, 0)),
            pl.BlockSpec((None, 1, D_MODEL), lambda i, j: (layer, 0, 0)),
            pl.BlockSpec((None, D_MODEL, W_IN_OUT), lambda i, j: (layer, 0, 0)),
        ],
        out_specs=[bs3(W_CONV_OUT), bs3(W_MLA_OUT), bs3(W_GLA_OUT)],
        out_shape=[jax.ShapeDtypeStruct((b, t, W_CONV_OUT), F32),
                   jax.ShapeDtypeStruct((b, t, W_MLA_OUT), F32),
                   jax.ShapeDtypeStruct((b, t, W_GLA_OUT), F32)],
        compiler_params=_params(("arbitrary", "arbitrary"), 48),
        name="in_proj",
    )(x, mod, g_mix, w_in_p)


def _segsum(x2, p_ref):
    return _dot(x2.astype(BF16), p_ref[...])


def _swap_rope_halves(x):
    lane = lax.broadcasted_iota(jnp.int32, x.shape, 1)
    return jnp.where(lane < 80, pltpu.roll(x, LANES - 16, 1), pltpu.roll(x, 16, 1))


def _qk_kernel(mla_ref, cos_ref, sin_ref, gqa_ref, wuq_ref, gq_ref, gkva_ref, wuk_ref, gk_ref, gkr_ref,
               p_ref, invn_ref, q_out, k_out, ckv_out, kr_out, *, q_scale):
    bb, tt, _ = mla_ref.shape
    rows = bb * tt
    mla = mla_ref[...].reshape(rows, W_MLA_OUT)
    cos = cos_ref[...]
    sin = sin_ref[...]
    invn = invn_ref[...]

    def head_norm_rope(xh, gain):
        rstd = lax.rsqrt(_segsum(xh * xh, p_ref) * invn + EPS)
        xn = xh * rstd * gain
        return xn * cos + _swap_rope_halves(xn) * sin

    q_lat = mla[:, 0:MLA_Q_LORA]
    qa = q_lat * lax.rsqrt(jnp.mean(q_lat * q_lat, axis=-1, keepdims=True) + EPS) * gqa_ref[...]
    q = _dot(qa.astype(BF16), wuq_ref[...])
    kv_lat = mla[:, MLA_Q_LORA:MLA_Q_LORA + MLA_KV_LORA]
    ckv = kv_lat * lax.rsqrt(jnp.mean(kv_lat * kv_lat, axis=-1, keepdims=True) + EPS) * gkva_ref[...]
    ckv_out[...] = ckv.reshape(bb, tt, MLA_KV_LORA)
    kr = head_norm_rope(mla[:, MLA_Q_LORA + MLA_KV_LORA:], gkr_ref[...])
    kr_out[...] = kr.reshape(bb, tt, LANES)
    k = _dot(ckv.astype(BF16), wuk_ref[...])
    for h in range(MLA_HEADS):
        qh = head_norm_rope(q[:, h * HEAD_W:(h + 1) * HEAD_W], gq_ref[...]) * q_scale
        q_out[:, h, :, :] = qh.astype(q_out.dtype).reshape(bb, tt, HEAD_W)
        kh = k[:, h * HEAD_W:(h + 1) * HEAD_W]
        rstd = lax.rsqrt(_segsum(kh * kh, p_ref) * invn + EPS)
        k_out[:, h, :, :] = (kh * rstd * gk_ref[...] + kr).astype(k_out.dtype).reshape(bb, tt, HEAD_W)


def _qk_call(mla, cos_t, sin_t, tab_index, wp, layer, bb, tt, qk_dtype, q_scale):
    b, t, _ = mla.shape
    rows = bb * tt
    vec = lambda arr: pl.BlockSpec((None, 1, arr.shape[-1]), lambda i, j: (layer, 0, 0))
    mat = lambda arr: pl.BlockSpec((None,) + arr.shape[1:], lambda i, j: (layer, 0, 0))
    tab = pl.BlockSpec((rows, LANES), lambda i, j: (tab_index(i, j), 0))
    const2 = lambda arr: pl.BlockSpec(arr.shape, lambda i, j: (0, 0))
    hk = pl.BlockSpec((bb, MLA_HEADS, tt, HEAD_W), lambda i, j: (i, 0, j, 0))
    return pl.pallas_call(
        functools.partial(_qk_kernel, q_scale=q_scale),
        grid=(b // bb, t // tt),
        in_specs=[pl.BlockSpec((bb, tt, W_MLA_OUT), lambda i, j: (i, j, 0)), tab, tab,
                  vec(wp["g_qa"]), mat(wp["w_uq"]), vec(wp["gq_lane"]), vec(wp["g_kva"]), mat(wp["w_uk"]),
                  vec(wp["gk_lane"]), vec(wp["gkr_lane"]), const2(wp["p_seg"]), const2(wp["invn"])],
        out_specs=[hk, hk,
                   pl.BlockSpec((bb, tt, MLA_KV_LORA), lambda i, j: (i, j, 0)),
                   pl.BlockSpec((bb, tt, LANES), lambda i, j: (i, j, 0))],
        out_shape=[jax.ShapeDtypeStruct((b, MLA_HEADS, t, HEAD_W), qk_dtype),
                   jax.ShapeDtypeStruct((b, MLA_HEADS, t, HEAD_W), qk_dtype),
                   jax.ShapeDtypeStruct((b, t, MLA_KV_LORA), F32),
                   jax.ShapeDtypeStruct((b, t, LANES), F32)],
        compiler_params=_params(("arbitrary", "arbitrary"), 40),
        name="mla_prep",
    )(mla, cos_t, sin_t, wp["g_qa"], wp["w_uq"], wp["gq_lane"], wp["g_kva"], wp["w_uk"], wp["gk_lane"],
      wp["gkr_lane"], wp["p_seg"], wp["invn"])


def _attn_prompt_kernel(q_ref, k_ref, vt_ref, wuvt_ref, o_ref, m_s, l_s, acc_s, *, tq):
    qi = pl.program_id(1)
    hq = MLA_HEADS * tq
    m_s[...] = jnp.full(m_s.shape, -jnp.inf, F32)
    l_s[...] = jnp.zeros(l_s.shape, F32)
    acc_s[...] = jnp.zeros(acc_s.shape, F32)

    def block(kb, masked):
        off = pl.multiple_of(kb * tq, tq)
        st = jnp.concatenate([_dot_nt(k_ref[h, pl.ds(off, tq), :], q_ref[h]) for h in range(MLA_HEADS)],
                             axis=1)
        if masked:
            key = lax.broadcasted_iota(jnp.int32, (tq, hq), 0)
            qry = lax.rem(lax.broadcasted_iota(jnp.int32, (tq, hq), 1), tq)
            st = jnp.where(key <= qry, st, -jnp.inf)
        m_old = m_s[...]
        m_new = jnp.maximum(m_old, jnp.max(st, axis=0, keepdims=True))
        p = jnp.exp2(st - m_new)
        alpha = jnp.exp2(m_old - m_new)
        m_s[...] = m_new
        l_s[...] = alpha * l_s[...] + jnp.sum(p, axis=0, keepdims=True)
        acc_s[...] = alpha * acc_s[...] + _dot(vt_ref[kb], p.astype(BF16))

    def body(kb, _):
        block(kb, False)
        return 0

    lax.fori_loop(0, qi, body, 0)
    block(qi, True)
    ot = (acc_s[...] / l_s[...]).astype(BF16)
    yt = jnp.zeros((MLA_HEADS * MLA_V, tq), F32)
    for h in range(MLA_HEADS):
        yt = yt + _dot(wuvt_ref[h], ot[:, h * tq:(h + 1) * tq])
    o_ref[...] = jnp.transpose(yt)


def _attn_prompt_call(q_cat, k_cat, ckv_t, w_uvpt, layer, tq):
    b, _, t, _ = q_cat.shape
    nkb = t // tq
    return pl.pallas_call(
        functools.partial(_attn_prompt_kernel, tq=tq),
        grid=(b, nkb),
        in_specs=[
            pl.BlockSpec((None, MLA_HEADS, tq, HEAD_W), lambda i, j: (i, 0, j, 0)),
            pl.BlockSpec((None, MLA_HEADS, t, HEAD_W), lambda i, j: (i, 0, 0, 0)),
            pl.BlockSpec((None, nkb, MLA_KV_LORA, tq), lambda i, j: (i, 0, 0, 0)),
            pl.BlockSpec((None, MLA_HEADS, MLA_HEADS * MLA_V, MLA_KV_LORA), lambda i, j: (layer, 0, 0, 0)),
        ],
        out_specs=pl.BlockSpec((None, tq, MLA_HEADS * MLA_V), lambda i, j: (i, j, 0)),
        out_shape=jax.ShapeDtypeStruct((b, t, MLA_HEADS * MLA_V), F32),
        scratch_shapes=[pltpu.VMEM((1, MLA_HEADS * tq), F32), pltpu.VMEM((1, MLA_HEADS * tq), F32),
                        pltpu.VMEM((MLA_KV_LORA, MLA_HEADS * tq), F32)],
        compiler_params=_params(("arbitrary", "arbitrary"), 48),
        name="attn_prompt",
    )(q_cat, k_cat, ckv_t, w_uvpt)


def _attn_sample_kernel(pt_ref, q_ref, ckvn_ref, krn_ref, wabs_ref, wukt_ref, wuv_ref,
                        cache_ckv, cache_krt, o_ref, ckv_buf, krt_buf, s_buf, sem,
                        *, layer, n_pages, chunk, ts):
    b = pl.program_id(0)
    nb = pl.num_programs(0)
    past = n_pages * PAGE_SIZE
    hq = MLA_HEADS * ts
    ppc = chunk // PAGE_SIZE
    n_chunks = past // chunk

    def page_copies(seq, slot, j):
        page = pt_ref[seq, j]
        ckv_dst = ckv_buf.at[slot, pl.ds(j * PAGE_SIZE, PAGE_SIZE), :]
        krt_dst = krt_buf.at[slot, j // ppc, :, pl.ds((j % ppc) * PAGE_SIZE, PAGE_SIZE)]
        return (pltpu.make_async_copy(cache_ckv.at[layer, page], ckv_dst, sem.at[slot, 0]),
                pltpu.make_async_copy(cache_krt.at[layer, page], krt_dst, sem.at[slot, 1]))

    def start_seq(seq, slot):
        for j in range(n_pages):
            c0, c1 = page_copies(seq, slot, j)
            c0.start()
            c1.start()

    def wait_seq(seq, slot):
        for j in range(n_pages):
            c0, c1 = page_copies(seq, slot, j)
            c0.wait()
            c1.wait()

    slot = lax.rem(b, 2)

    @pl.when(b == 0)
    def _():
        start_seq(0, 0)

    @pl.when(b + 1 < nb)
    def _():
        start_seq(b + 1, 1 - slot)

    wait_seq(b, slot)

    q_all = q_ref[...].reshape(hq, HEAD_W)
    q_wide = _dot(q_all.astype(BF16), wabs_ref[...])
    q_abs = jnp.concatenate(
        [q_wide[h * ts:(h + 1) * ts, h * MLA_KV_LORA:(h + 1) * MLA_KV_LORA] for h in range(MLA_HEADS)],
        axis=0).astype(BF16)
    q_rope = q_all[:, MLA_NOPE:MLA_NOPE + MLA_ROPE].astype(BF16)
    w_stack = jnp.concatenate([wukt_ref[...], q_abs], axis=0)
    nk = MLA_HEADS * MLA_NOPE

    def scores(c, rope_scores):
        res = _dot_nt(w_stack, c)
        k2 = res[0:nk, :] * res[0:nk, :]
        r_rows = []
        for h in range(MLA_HEADS):
            ssq = jnp.sum(k2[h * MLA_NOPE:(h + 1) * MLA_NOPE, :], axis=0, keepdims=True)
            r = lax.rsqrt(ssq * (1.0 / MLA_NOPE) + EPS)
            r_rows.append(jnp.broadcast_to(r, (ts, r.shape[1])))
        return res[nk:, :] * jnp.concatenate(r_rows, axis=0) + rope_scores

    def fold_lanes(x, op):
        out = x[:, 0:LANES]
        for i in range(1, x.shape[1] // LANES):
            out = op(out, x[:, i * LANES:(i + 1) * LANES])
        return out

    unroll = 2 if n_chunks % 2 == 0 else 1

    def pass1(it, m_part):
        for u in range(unroll):
            ci = it * unroll + u
            off = pl.multiple_of(ci * chunk, chunk)
            c = ckv_buf[slot, pl.ds(off, chunk), :].astype(BF16)
            s = scores(c, _dot(q_rope, krt_buf[slot, ci].astype(BF16)))
            s_buf[ci] = s
            m_part = jnp.maximum(m_part, fold_lanes(s, jnp.maximum))
        return m_part

    m_part = lax.fori_loop(0, n_chunks // unroll, pass1, jnp.full((hq, LANES), -jnp.inf, F32))
    c_new = jnp.concatenate([ckvn_ref[...], jnp.zeros((LANES - ts, MLA_KV_LORA), F32)], axis=0).astype(BF16)
    kr_new = jnp.concatenate([krn_ref[...], jnp.zeros((LANES - ts, MLA_ROPE), F32)], axis=0).astype(BF16)
    s_new = scores(c_new, _dot_nt(q_rope, kr_new))
    jcol = lax.broadcasted_iota(jnp.int32, (hq, LANES), 1)
    trow = lax.rem(lax.broadcasted_iota(jnp.int32, (hq, LANES), 0), ts)
    s_new = jnp.where(jcol <= trow, s_new, -jnp.inf)
    m = jnp.maximum(jnp.max(m_part, axis=1, keepdims=True), jnp.max(s_new, axis=1, keepdims=True))

    def pass2(it, carry):
        acc, l_part = carry
        for u in range(unroll):
            ci = it * unroll + u
            off = pl.multiple_of(ci * chunk, chunk)
            p = jnp.exp(s_buf[ci] - m)
            acc = acc + _dot(p.astype(BF16), ckv_buf[slot, pl.ds(off, chunk), :].astype(BF16))
            l_part = l_part + fold_lanes(p, jnp.add)
        return acc, l_part

    acc, l_part = lax.fori_loop(0, n_chunks // unroll, pass2,
                                (jnp.zeros((hq, MLA_KV_LORA), F32), jnp.zeros((hq, LANES), F32)))
    p_new = jnp.exp(s_new - m)
    acc = acc + _dot(p_new.astype(BF16), c_new)
    l = jnp.sum(l_part, axis=1, keepdims=True) + jnp.sum(p_new, axis=1, keepdims=True)
    o = (acc / l).astype(BF16)
    y = jnp.zeros((ts, MLA_HEADS * MLA_V), F32)
    for h in range(MLA_HEADS):
        y = y + _dot(o, wuv_ref[h])[h * ts:(h + 1) * ts, :]
    o_ref[...] = y


def _attn_sample_call(page_table, q_cat, ckv_new, kr_new, wp, cache_ckv, cache_krt, layer, chunk):
    b, _, ts, _ = q_cat.shape
    n_pages = page_table.shape[1]
    past = n_pages * PAGE_SIZE
    hq = MLA_HEADS * ts
    kern = functools.partial(_attn_sample_kernel, layer=layer, n_pages=n_pages, chunk=chunk, ts=ts)
    lay3 = lambda arr: pl.BlockSpec((None,) + arr.shape[1:], lambda i, pt: (layer,) + (0,) * (arr.ndim - 1))
    grid_spec = pltpu.PrefetchScalarGridSpec(
        num_scalar_prefetch=1,
        grid=(b,),
        in_specs=[
            pl.BlockSpec((None, MLA_HEADS, ts, HEAD_W), lambda i, pt: (i, 0, 0, 0)),
            pl.BlockSpec((None, ts, MLA_KV_LORA), lambda i, pt: (i, 0, 0)),
            pl.BlockSpec((None, ts, MLA_ROPE), lambda i, pt: (i, 0, 0)),
            lay3(wp["w_abs"]), lay3(wp["w_uk_t"]),
            lay3(wp["w_uvp"]),
            pl.BlockSpec(memory_space=pl.ANY),
            pl.BlockSpec(memory_space=pl.ANY),
        ],
        out_specs=pl.BlockSpec((None, ts, MLA_HEADS * MLA_V), lambda i, pt: (i, 0, 0)),
        scratch_shapes=[
            pltpu.VMEM((2, past, MLA_KV_LORA), F32),
            pltpu.VMEM((2, past // chunk, MLA_ROPE, chunk), F32),
            pltpu.VMEM((past // chunk, hq, chunk), F32),
            pltpu.SemaphoreType.DMA((2, 2)),
        ],
    )
    return pl.pallas_call(
        kern,
        grid_spec=grid_spec,
        out_shape=jax.ShapeDtypeStruct((b, ts, MLA_HEADS * MLA_V), F32),
        compiler_params=_params(("arbitrary",), 56),
        name="attn_sample",
    )(page_table, q_cat, ckv_new, kr_new, wp["w_abs"], wp["w_uk_t"], wp["w_uvp"], cache_ckv, cache_krt)


def _cumsum_rows(x):
    n = x.shape[0]
    row = lax.broadcasted_iota(jnp.int32, x.shape, 0)
    d = 1
    while d < n:
        x = x + jnp.where(row >= d, pltpu.roll(x, d, 0), 0.0)
        d *= 2
    return x


def _seq_kernel(conv_ref, gla_ref, cprev_ref, s0_ref, wconv_ref, wa2_ref, ba_ref, go_ref,
                segexp_ref, bdmask_ref, p256_ref,
                y_ref, cstate_ref, sstate_ref, s_scr, c_scr, v_scr):
    ci = pl.program_id(1)
    nci = pl.num_programs(1)
    ln = conv_ref.shape[0]
    kd = GLA_HEADS * GLA_DK
    vd = GLA_HEADS * GLA_DV

    @pl.when(ci == 0)
    def _():
        c_scr[...] = cprev_ref[...]
        s_scr[...] = jnp.zeros(s_scr.shape, F32)
        for h in range(GLA_HEADS):
            s_scr[h * GLA_DK:(h + 1) * GLA_DK, h * GLA_DV:(h + 1) * GLA_DV] = s0_ref[h]

    conv = conv_ref[...]
    v = conv[:, CONV_DIM:2 * CONV_DIM] * conv[:, 0:CONV_DIM]
    row = lax.broadcasted_iota(jnp.int32, (ln, CONV_DIM), 0)
    prev = c_scr[...]
    p0 = jnp.broadcast_to(prev[0:1, :], (ln, CONV_DIM))
    p1 = jnp.broadcast_to(prev[1:2, :], (ln, CONV_DIM))
    v1 = jnp.where(row == 0, p1, pltpu.roll(v, 1, 0))
    v2 = jnp.where(row == 0, p0, jnp.where(row == 1, p1, pltpu.roll(v, 2, 0)))
    wc = wconv_ref[...]
    y_a = conv[:, 2 * CONV_DIM:] * (v2 * wc[0:1, :] + v1 * wc[1:2, :] + v * wc[2:3, :])
    v_scr[...] = v
    c_scr[...] = v_scr[ln - 2:ln, :]
    y_ref[:, 0:CONV_DIM] = y_a

    gla = gla_ref[...]
    gq = gla[:, 0:kd] * (GLA_DK ** -0.5)
    gk = gla[:, kd:2 * kd]
    gv = gla[:, 2 * kd:2 * kd + vd]
    ga = gla[:, 2 * kd + vd:2 * kd + vd + LANES]
    gr = gla[:, 2 * kd + vd + LANES:]
    x = _dot(ga.astype(BF16), wa2_ref[...]) + ba_ref[...]
    log_a = (jnp.minimum(x, 0.0) - jnp.log(1.0 + jnp.exp(-jnp.abs(x)))) * (1.0 / GLA_NORMALIZER)
    cum = _cumsum_rows(log_a)
    s_prev = s_scr[...]
    o = _dot((gq * jnp.exp(cum)).astype(BF16), s_prev.astype(BF16))
    segexp = segexp_ref[...]
    rowk = lax.broadcasted_iota(jnp.int32, (ln, kd), 0)
    o_intra = [jnp.zeros((ln - g0, vd), F32) for g0 in range(0, ln, SUBLANES)]
    for s in range(ln):
        g = s // SUBLANES
        g0 = g * SUBLANES
        dec = jnp.exp(jnp.minimum(cum[g0:, :] - cum[s:s + 1, :], 0.0))
        dmat = jnp.where(rowk[g0:, :] >= s, gq[g0:, :] * gk[s:s + 1, :] * dec, 0.0)
        att = _dot(dmat.astype(BF16), segexp)
        o_intra[g] = o_intra[g] + att * gv[s:s + 1, :]
    intra = jnp.zeros((ln, vd), F32)
    for g, part in enumerate(o_intra):
        g0 = g * SUBLANES
        if g0 == 0:
            intra = intra + part
        else:
            intra = intra + jnp.concatenate([jnp.zeros((g0, vd), F32), part], axis=0)
    o = o + intra
    last = cum[ln - 1:ln, :]
    kdec = gk * jnp.exp(last - cum)
    e_col = jnp.transpose(jnp.broadcast_to(jnp.exp(last), (kd, kd)))
    e_col = jnp.concatenate([e_col, e_col], axis=1)
    s_new = e_col * s_prev + bdmask_ref[...] * _dot_tn(kdec.astype(BF16), gv.astype(BF16))
    s_scr[...] = s_new
    rstd = lax.rsqrt(_dot((o * o).astype(BF16), p256_ref[...]) * (1.0 / GLA_DV) + EPS)
    y_ref[:, CONV_DIM:] = o * rstd * go_ref[...] * (gr * _sigmoid(gr))

    @pl.when(ci == nci - 1)
    def _():
        cstate_ref[...] = c_scr[...]
        for h in range(GLA_HEADS):
            sstate_ref[h] = s_scr[h * GLA_DK:(h + 1) * GLA_DK, h * GLA_DV:(h + 1) * GLA_DV]


def _seq_call(conv, gla, conv_prev, s0, wp, layer, ln):
    b, t, _ = conv.shape
    kd = GLA_HEADS * GLA_DK
    vd = GLA_HEADS * GLA_DV
    lay = lambda arr: pl.BlockSpec((None,) + arr.shape[1:], lambda i, j: (layer,) + (0,) * (arr.ndim - 1))
    const2 = lambda arr: pl.BlockSpec(arr.shape, lambda i, j: (0, 0))
    return pl.pallas_call(
        _seq_kernel,
        grid=(b, t // ln),
        in_specs=[
            pl.BlockSpec((None, ln, W_CONV_OUT), lambda i, j: (i, j, 0)),
            pl.BlockSpec((None, ln, W_GLA_OUT), lambda i, j: (i, j, 0)),
            pl.BlockSpec((None, CONV_W - 1, CONV_DIM), lambda i, j: (i, 0, 0)),
            pl.BlockSpec((None, GLA_HEADS, GLA_DK, GLA_DV), lambda i, j: (i, 0, 0, 0)),
            lay(wp["w_conv"]), lay(wp["w_a2p"]), lay(wp["b_a"]), lay(wp["g_gla_lane"]),
            const2(wp["seg_exp"]), const2(wp["bd_mask"]), const2(wp["p256"]),
        ],
        out_specs=[
            pl.BlockSpec((None, ln, CONV_DIM + vd), lambda i, j: (i, j, 0)),
            pl.BlockSpec((None, CONV_W - 1, CONV_DIM), lambda i, j: (i, 0, 0)),
            pl.BlockSpec((None, GLA_HEADS, GLA_DK, GLA_DV), lambda i, j: (i, 0, 0, 0)),
        ],
        out_shape=[jax.ShapeDtypeStruct((b, t, CONV_DIM + vd), F32),
                   jax.ShapeDtypeStruct((b, CONV_W - 1, CONV_DIM), F32),
                   jax.ShapeDtypeStruct((b, GLA_HEADS, GLA_DK, GLA_DV), F32)],
        scratch_shapes=[pltpu.VMEM((kd, vd), F32), pltpu.VMEM((CONV_W - 1, CONV_DIM), F32),
                        pltpu.VMEM((ln, CONV_DIM), F32)],
        compiler_params=_params(("arbitrary", "arbitrary"), 40),
        name="conv_gla",
    )(conv, gla, conv_prev, s0, wp["w_conv"], wp["w_a2p"], wp["b_a"], wp["g_gla_lane"],
      wp["seg_exp"], wp["bd_mask"], wp["p256"])


def _store_token_tiles(ref, x):
    rows = x.shape[0]
    for s in range(TOKEN_TILE):
        ref[pl.ds(s, rows, stride=TOKEN_TILE), :] = x[:, s * LANES:(s + 1) * LANES]


def _load_token_tiles(ref, base, rows):
    return jnp.concatenate(
        [ref[pl.ds(base + s, rows, stride=TOKEN_TILE), :] for s in range(TOKEN_TILE)], axis=1)


def _split_bf16(x):
    hi = x.astype(BF16)
    lo = (x - hi.astype(F32)).astype(BF16)
    return hi, lo


def _out_kernel(yac_ref, yb_ref, x_ref, mod_ref, woac_ref, wob_ref, g_ref, wrh_ref, wrl_ref, br_ref,
                x1_ref, h_ref, idx_ref, gate_ref):
    bb, tt, _ = x_ref.shape
    rows = bb * tt
    yac = yac_ref[...].reshape(rows, 2 * CONV_DIM).astype(BF16)
    yb = yb_ref[...].reshape(rows, MLA_HEADS * MLA_V).astype(BF16)
    y = (_dot(yac, woac_ref[...]) + _dot(yb, wob_ref[...])).reshape(bb, tt, D_MODEL)
    x1 = x_ref[...] + mod_ref[:, 2:3, :] * y
    x1_ref[...] = x1
    ms = jnp.mean(x1 * x1, axis=-1, keepdims=True)
    h = x1 * lax.rsqrt(ms + EPS) * g_ref[...]
    h = (h * (1.0 + mod_ref[:, 4:5, :]) + mod_ref[:, 3:4, :]).reshape(rows, D_MODEL)
    _store_token_tiles(h_ref, h)
    h_hi, h_lo = _split_bf16(h)
    logits = _dot(h_hi, wrh_ref[...]) + (_dot(h_hi, wrl_ref[...]) + _dot(h_lo, wrh_ref[...])) + br_ref[...]
    lane = lax.broadcasted_iota(jnp.int32, (rows, LANES), 1)
    lane_f = lane.astype(F32)
    idx_out = jnp.zeros((rows, LANES), F32)
    val_out = jnp.zeros((rows, LANES), F32)
    vals = []
    for k in range(TOP_K):
        mx = jnp.max(logits, axis=-1, keepdims=True)
        ix = jnp.min(jnp.where(logits == mx, lane_f, float(LANES)), axis=-1, keepdims=True)
        idx_out = jnp.where(lane == k, ix, idx_out)
        vals.append(mx)
        logits = jnp.where(lane_f == ix, -jnp.inf, logits)
    es = [jnp.exp(v - vals[0]) for v in vals]
    den = es[0] + es[1] + es[2] + es[3]
    for k in range(TOP_K):
        val_out = jnp.where(lane == k, es[k] / den, val_out)
    idx_ref[...] = idx_out.astype(jnp.int32)
    gate_ref[...] = val_out


def _out_call(yac, yb, x, mod, wp, layer, bb, tt):
    b, t, _ = x.shape
    n = b * t
    rows = bb * tt
    nj = t // tt
    bs3 = lambda w: pl.BlockSpec((bb, tt, w), lambda i, j: (i, j, 0))
    lay = lambda arr: pl.BlockSpec((None,) + arr.shape[1:], lambda i, j: (layer,) + (0,) * (arr.ndim - 1))
    flat = lambda w: pl.BlockSpec((rows, w), lambda i, j: (i * nj + j, 0))
    return pl.pallas_call(
        _out_kernel,
        grid=(b // bb, nj),
        in_specs=[bs3(2 * CONV_DIM), bs3(MLA_HEADS * MLA_V), bs3(D_MODEL),
                  pl.BlockSpec((bb, N_MOD, D_MODEL), lambda i, j: (i, 0, 0)),
                  lay(wp["w_o_ac"]), lay(wp["w_o_b"]), lay(wp["g_ffn"]),
                  lay(wp["w_r_hi"]), lay(wp["w_r_lo"]), lay(wp["b_r"])],
        out_specs=[bs3(D_MODEL), pl.BlockSpec((rows * TOKEN_TILE, LANES), lambda i, j: (i * nj + j, 0)),
                   flat(LANES), flat(LANES)],
        out_shape=[jax.ShapeDtypeStruct((b, t, D_MODEL), F32),
                   jax.ShapeDtypeStruct((n * TOKEN_TILE, LANES), F32),
                   jax.ShapeDtypeStruct((n, LANES), jnp.int32),
                   jax.ShapeDtypeStruct((n, LANES), F32)],
        compiler_params=_params(("arbitrary", "arbitrary"), 48),
        name="out_proj_router",
    )(yac, yb, x, mod, wp["w_o_ac"], wp["w_o_b"], wp["g_ffn"], wp["w_r_hi"], wp["w_r_lo"], wp["b_r"])


def _row_dma_loop(n_rows, make_copy, start):
    def body(g, _):
        for u in range(DMA_UNROLL):
            cp = make_copy(g * DMA_UNROLL + u)
            if start:
                cp.start()
            else:
                cp.wait()
        return 0
    lax.fori_loop(0, n_rows // DMA_UNROLL, body, 0)


def _experts_kernel(stok_ref, be_ref, src0_ref, nrows_ref, nv_ref, h_hbm, h2d_hbm, wg_ref, bg_ref, wu_ref, bu_ref,
                    wd_ref, bd_ref, o_ref, xbuf, wg_s, wu_s, wd_s, sem):
    i = pl.program_id(0)
    nv = nv_ref[0]

    def start_gather(blk, slot):
        first = src0_ref[blk]
        last = first + nrows_ref[blk] - 1

        def make_copy(r):
            tok = stok_ref[jnp.minimum(first + r, last)]
            dst = xbuf.at[slot, pl.ds(pl.multiple_of(r * TOKEN_TILE, TOKEN_TILE), TOKEN_TILE), :]
            return pltpu.make_async_copy(h_hbm.at[tok], dst, sem.at[slot])

        _row_dma_loop(MOE_TM, make_copy, True)

    def wait_gather(slot):
        pltpu.make_async_copy(h2d_hbm.at[pl.ds(0, MOE_TM * TOKEN_TILE)], xbuf.at[slot], sem.at[slot]).wait()

    slot = lax.rem(i, 2)

    @pl.when(i == 0)
    def _():
        start_gather(0, 0)

    @pl.when(i + 1 < nv)
    def _():
        start_gather(i + 1, 1 - slot)

    @pl.when(i >= nv)
    def _():
        o_ref[...] = jnp.zeros(o_ref.shape, F32)

    @pl.when(i < nv)
    def _():
        wait_gather(slot)
        prev = be_ref[jnp.maximum(i - 1, 0)]

        @pl.when((i == 0) | (be_ref[i] != prev))
        def _():
            wg_s[...] = wg_ref[...].astype(BF16)
            wu_s[...] = wu_ref[...].astype(BF16)
            wd_s[...] = wd_ref[...].astype(BF16)

        x = _load_token_tiles(xbuf.at[slot], 0, MOE_TM).astype(BF16)
        g = _dot(x, wg_s[...]) + bg_ref[...]
        u = _dot(x, wu_s[...]) + bu_ref[...]
        g = jnp.minimum(g, SWIGLU_LIMIT)
        u = jnp.clip(u, -SWIGLU_LIMIT, SWIGLU_LIMIT)
        a = (u + 1.0) * (g * _sigmoid(SWIGLU_ALPHA * g))
        _store_token_tiles(o_ref, _dot(a.astype(BF16), wd_s[...]) + bd_ref[...])


def _experts_call(rt, h_tiles, w_gate, b_gate, w_up, b_up, w_down, b_down, layer, p_rows):
    nblk = p_rows // MOE_TM
    eff = lambda i, nv: jnp.minimum(i, nv[0] - 1)
    wspec = pl.BlockSpec((None, None, D_MODEL, D_FF),
                         lambda i, st, be, s0, nr, nv: (layer, be[eff(i, nv)], 0, 0))
    bspec = pl.BlockSpec((None, None, 1, D_FF), lambda i, st, be, s0, nr, nv: (layer, be[eff(i, nv)], 0, 0))
    grid_spec = pltpu.PrefetchScalarGridSpec(
        num_scalar_prefetch=5,
        grid=(nblk,),
        in_specs=[pl.BlockSpec(memory_space=pl.ANY), pl.BlockSpec(memory_space=pl.ANY),
                  wspec, bspec, wspec, bspec, wspec, bspec],
        out_specs=pl.BlockSpec((MOE_TM * TOKEN_TILE, LANES), lambda i, st, be, s0, nr, nv: (i, 0)),
        scratch_shapes=[pltpu.VMEM((2, MOE_TM * TOKEN_TILE, LANES), F32),
                        pltpu.VMEM((D_MODEL, D_FF), BF16), pltpu.VMEM((D_MODEL, D_FF), BF16),
                        pltpu.VMEM((D_FF, D_MODEL), BF16), pltpu.SemaphoreType.DMA((2,))],
    )
    depth = w_gate.shape[0]
    b4 = lambda bias: bias.reshape(depth, N_EXPERTS, 1, bias.shape[-1])
    return pl.pallas_call(
        _experts_kernel,
        grid_spec=grid_spec,
        out_shape=jax.ShapeDtypeStruct((p_rows * TOKEN_TILE, LANES), F32),
        compiler_params=_params(("arbitrary",), 56),
        name="moe_experts",
    )(rt["stok"], rt["blk_e"], rt["src0"], rt["nrows"], rt["n_valid"],
      h_tiles.reshape(-1, TOKEN_TILE, LANES), h_tiles.reshape(-1, LANES),
      w_gate, b4(b_gate), w_up, b4(b_up), w_down, b4(b_down))


def _combine_kernel(pos_ref, posn_ref, gate_ref, x1_ref, mod_ref, yb_hbm, yb2d_hbm, o_ref, buf, sem):
    i = pl.program_id(0)
    bb, tt, _ = x1_ref.shape
    rows = bb * tt
    slot = lax.rem(i, 2)

    def start_gather(p_ref, slot):
        def make_copy(r):
            dst = buf.at[slot, pl.ds(pl.multiple_of(r * TOKEN_TILE, TOKEN_TILE), TOKEN_TILE), :]
            return pltpu.make_async_copy(yb_hbm.at[p_ref[0, 0, r]], dst, sem.at[slot])
        _row_dma_loop(rows * TOP_K, make_copy, True)

    @pl.when(i == 0)
    def _():
        start_gather(pos_ref, 0)

    @pl.when(i + 1 < pl.num_programs(0))
    def _():
        start_gather(posn_ref, 1 - slot)

    pltpu.make_async_copy(yb2d_hbm.at[pl.ds(0, rows * TOP_K * TOKEN_TILE)], buf.at[slot], sem.at[slot]).wait()
    gates = gate_ref[...]
    tile = buf.at[slot]
    y = _load_token_tiles(tile, 0, rows) * gates[:, 0:1]
    for k in range(1, TOP_K):
        y = y + _load_token_tiles(tile, k * rows * TOKEN_TILE, rows) * gates[:, k:k + 1]
    o_ref[...] = x1_ref[...] + mod_ref[:, 5:6, :] * y.reshape(bb, tt, D_MODEL)


def _combine_call(pos, gates, x1, mod, yb, tok0, bb, tt):
    b, t, _ = x1.shape
    rows = bb * tt
    nj = t // tt
    n_steps = (b // bb) * nj
    blk0 = tok0 // rows
    n_all = pos.shape[0] // TOP_K
    pos3 = jnp.swapaxes(pos.reshape(n_all // rows, rows, TOP_K), 1, 2).reshape(n_all // rows, 1, rows * TOP_K)
    pos_spec = lambda nxt: pl.BlockSpec(
        (1, 1, rows * TOP_K), lambda i: (blk0 + jnp.minimum(i + nxt, n_steps - 1), 0, 0), memory_space=pltpu.SMEM)
    return pl.pallas_call(
        _combine_kernel,
        grid=(n_steps,),
        in_specs=[
            pos_spec(0), pos_spec(1),
            pl.BlockSpec((rows, LANES), lambda i: (blk0 + i, 0)),
            pl.BlockSpec((bb, tt, D_MODEL), lambda i: (i // nj, i % nj, 0)),
            pl.BlockSpec((bb, N_MOD, D_MODEL), lambda i: (i // nj, 0, 0)),
            pl.BlockSpec(memory_space=pl.ANY),
            pl.BlockSpec(memory_space=pl.ANY),
        ],
        out_specs=pl.BlockSpec((bb, tt, D_MODEL), lambda i: (i // nj, i % nj, 0)),
        out_shape=jax.ShapeDtypeStruct((b, t, D_MODEL), F32),
        scratch_shapes=[pltpu.VMEM((2, TOP_K * rows * TOKEN_TILE, LANES), F32), pltpu.SemaphoreType.DMA((2,))],
        compiler_params=_params(("arbitrary",), 32),
        name="moe_combine",
    )(pos3, pos3, gates, x1, mod, yb.reshape(-1, TOKEN_TILE, LANES), yb.reshape(-1, LANES))


def _route(top_i, n_tokens):
    m = n_tokens * TOP_K
    flat_e = top_i.reshape(m)
    onehot = (flat_e[:, None] == jnp.arange(N_EXPERTS, dtype=jnp.int32)[None, :]).astype(jnp.int32)
    csum = jnp.cumsum(onehot, axis=0)
    rank = jnp.sum(onehot * csum, axis=1) - 1
    counts = csum[-1]
    padded = (counts + MOE_TM - 1) // MOE_TM * MOE_TM
    pends = jnp.cumsum(padded)
    pstarts = pends - padded
    pos = jnp.sum(onehot * pstarts[None, :], axis=1) + rank
    p_rows = (m // MOE_TM + N_EXPERTS) * MOE_TM
    nblk = p_rows // MOE_TM
    blk_start = jnp.arange(nblk, dtype=jnp.int32) * MOE_TM
    blk_e = jnp.minimum(jnp.sum((blk_start[:, None] >= pends[None, :]).astype(jnp.int32), axis=1), N_EXPERTS - 1)
    n_valid = (pends[-1] // MOE_TM).reshape(1)
    slot = jnp.arange(m, dtype=jnp.int32)
    stok = (jnp.sort(flat_e * m + slot) % m) // TOP_K
    starts = jnp.cumsum(counts) - counts
    off = blk_start - pstarts[blk_e]
    src0 = jnp.clip(starts[blk_e] + off, 0, m - 1)
    nrows = jnp.clip(counts[blk_e] - off, 1, MOE_TM)
    route = dict(pos=pos, blk_e=blk_e, n_valid=n_valid, stok=stok, src0=src0, nrows=nrows)
    return {k: v.astype(jnp.int32) for k, v in route.items()}, p_rows


def _prep_weights(g_mix, g_ffn, w_in, w_o, w_conv, g_qa, w_uq, g_qn, g_qr, g_kva, g_kr, w_uk, g_kn, w_uv,
                  w_a2, b_a, g_gla_o, w_router, b_router):
    depth = w_in.shape[0]
    offs = np.cumsum((0,) + IN_SIZES)
    seg = lambda i: w_in[:, :, offs[i]:offs[i + 1]]
    z = lambda n: jnp.zeros((depth, D_MODEL, n), w_in.dtype)
    w_in_p = jnp.concatenate(
        [seg(0), seg(1), seg(2), seg(3), seg(4), z(64), seg(5), z(32),
         seg(6), seg(7), seg(8), seg(9), z(LANES - GLA_GATE_RANK), seg(10)], axis=-1).astype(BF16)
    pad_last = lambda a, n: jnp.pad(a, [(0, 0)] * (a.ndim - 1) + [(0, n - a.shape[-1])])
    v3 = lambda a: a.reshape(depth, 1, a.shape[-1])
    wp = {}
    wp["g_mix"] = v3(g_mix)
    wp["g_ffn"] = v3(g_ffn)
    wp["w_in_p"] = w_in_p
    wp["g_qa"] = v3(g_qa)
    wp["w_uq"] = pad_last(w_uq, HEAD_W).reshape(depth, MLA_Q_LORA, MLA_HEADS * HEAD_W).astype(BF16)
    wp["gq_lane"] = v3(pad_last(jnp.concatenate([g_qn, g_qr], axis=-1), HEAD_W))
    wp["g_kva"] = v3(g_kva)
    wp["w_uk"] = pad_last(w_uk, HEAD_W).reshape(depth, MLA_KV_LORA, MLA_HEADS * HEAD_W).astype(BF16)
    wp["w_uk_t"] = jnp.swapaxes(w_uk.reshape(depth, MLA_KV_LORA, MLA_HEADS * MLA_NOPE), 1, 2).astype(BF16)
    wp["gk_lane"] = v3(pad_last(g_kn, HEAD_W))
    wp["gkr_lane"] = v3(pad_last(jnp.concatenate([jnp.zeros((depth, MLA_NOPE), F32), g_kr], axis=-1), HEAD_W))
    w_abs = jnp.transpose(w_uk, (0, 3, 2, 1)) * g_kn[:, :, None, None]
    w_abs = w_abs.reshape(depth, MLA_NOPE, MLA_HEADS * MLA_KV_LORA)
    wp["w_abs"] = jnp.pad(w_abs, ((0, 0), (0, HEAD_W - MLA_NOPE), (0, 0))).astype(BF16)
    eye_h = jnp.eye(MLA_HEADS, dtype=F32)
    wp["w_uvp"] = jnp.einsum("lrhv,hg->lhrgv", w_uv, eye_h).reshape(
        depth, MLA_HEADS, MLA_KV_LORA, MLA_HEADS * MLA_V).astype(BF16)
    wp["w_uvpt"] = jnp.swapaxes(wp["w_uvp"], 2, 3)
    wp["w_o_ac"] = jnp.concatenate([w_o[:, 0:CONV_DIM], w_o[:, CONV_DIM + MLA_HEADS * MLA_V:]], axis=1).astype(BF16)
    wp["w_o_b"] = w_o[:, CONV_DIM:CONV_DIM + MLA_HEADS * MLA_V].astype(BF16)
    wp["w_conv"] = w_conv
    wp["w_a2p"] = jnp.pad(w_a2, ((0, 0), (0, LANES - GLA_GATE_RANK), (0, 0))).astype(BF16)
    wp["b_a"] = v3(b_a)
    wp["g_gla_lane"] = v3(jnp.tile(g_gla_o, (1, GLA_HEADS)))
    w_r = pad_last(w_router, LANES)
    w_r_hi = w_r.astype(BF16)
    wp["w_r_hi"] = w_r_hi
    wp["w_r_lo"] = (w_r - w_r_hi.astype(F32)).astype(BF16)
    wp["b_r"] = v3(jnp.concatenate([b_router, jnp.full((depth, LANES - N_EXPERTS), NEG_BIG, F32)], axis=-1))
    lane = np.arange(LANES)
    seg_id = np.where(lane < MLA_NOPE, 0, np.where(lane < MLA_NOPE + MLA_ROPE, 1, 2 + lane))
    wp["p_seg"] = jnp.asarray((seg_id[:, None] == seg_id[None, :]).astype(np.float32), BF16)
    wp["invn"] = jnp.asarray(np.where(lane < MLA_NOPE, 1.0 / MLA_NOPE, 1.0 / MLA_ROPE).astype(np.float32))[None, :]
    kd = GLA_HEADS * GLA_DK
    vd = GLA_HEADS * GLA_DV
    hk = np.arange(kd) // GLA_DK
    hv = np.arange(vd) // GLA_DV
    bd = (hk[:, None] == hv[None, :]).astype(np.float32)
    wp["seg_exp"] = jnp.asarray(bd, BF16)
    wp["bd_mask"] = jnp.asarray(bd, F32)
    wp["p256"] = jnp.asarray((hv[:, None] == hv[None, :]).astype(np.float32), BF16)
    return wp


def _rope_tables(pos):
    half = MLA_ROPE // 2
    freqs = ROPE_THETA ** (-jnp.arange(half, dtype=F32) / half)
    ang = pos.astype(F32)[:, None] * freqs
    cos, sin = jnp.cos(ang), jnp.sin(ang)
    t = pos.shape[0]
    cos_t = jnp.concatenate([jnp.ones((t, MLA_NOPE), F32), cos, cos, jnp.zeros((t, HEAD_W - 96), F32)], axis=-1)
    sin_t = jnp.concatenate([jnp.zeros((t, MLA_NOPE), F32), -sin, sin, jnp.zeros((t, HEAD_W - 96), F32)], axis=-1)
    return cos_t, sin_t


def kernel(x_prompt, x_sample, cache_ckv, cache_kr, state_conv, state_gla, page_table, c_prompt, c_sample, g_mix, g_ffn, w_mod, b_mod, w_in, w_o, w_conv, g_qa, w_uq, g_qn, g_qr, g_kva, g_kr, w_uk, g_kn, w_uv, w_a2, b_a, g_gla_o, w_router, b_router, w_gate, b_gate, w_up, b_up, w_down, b_down):
    depth = w_in.shape[0]
    bp, tp, _ = x_prompt.shape
    bs, ts, _ = x_sample.shape
    n_pages = page_table.shape[1]
    past = n_pages * PAGE_SIZE
    n_p, n_s = bp * tp, bs * ts
    n_all = n_p + n_s

    tt_p = min(512, tp)
    bb_s = min(16, bs)
    tq = min(256, tp)
    ln_p = min(64, tp)
    chunk_s = min(1024, max(PAGE_SIZE, past // 2))
    rows_c = COMBINE_TOK

    wp = _prep_weights(g_mix, g_ffn, w_in, w_o, w_conv, g_qa, w_uq, g_qn, g_qr, g_kva, g_kr, w_uk, g_kn, w_uv,
                       w_a2, b_a, g_gla_o, w_router, b_router)
    cache_krt = jnp.swapaxes(cache_kr, 2, 3)

    nb_mod = -(-(bp + bs) // 16) * 16
    c_all = jnp.concatenate([c_prompt, c_sample, jnp.zeros((nb_mod - bp - bs, D_MODEL), F32)], axis=0)
    mod_all = _mod_call(c_all, w_mod, b_mod)

    cos_p, sin_p = _rope_tables(jnp.arange(tp, dtype=jnp.int32))
    cos_s, sin_s = _rope_tables(past + jnp.arange(ts, dtype=jnp.int32))
    cos_s, sin_s = jnp.tile(cos_s, (bb_s, 1)), jnp.tile(sin_s, (bb_s, 1))

    conv0 = jnp.zeros((bp, CONV_W - 1, CONV_DIM), F32)
    gla0 = jnp.zeros((bp, GLA_HEADS, GLA_DK, GLA_DV), F32)

    xp, xs = x_prompt, x_sample
    outs = {k: [] for k in ("ckv_p", "kr_p", "conv_p", "gla_p", "ckv_s", "kr_s", "conv_s", "gla_s")}
    for l in range(depth):
        mod_p = mod_all[l, 0:bp].reshape(bp, N_MOD, D_MODEL)
        mod_s = mod_all[l, bp:bp + bs].reshape(bs, N_MOD, D_MODEL)

        conv_o, mla_o, gla_o = _in_call(xp, mod_p, wp["g_mix"], wp["w_in_p"], l, 1, tt_p)
        q_cat, k_cat, ckv, kr128 = _qk_call(mla_o, cos_p, sin_p, lambda i, j: j, wp, l, 1, tt_p, BF16,
                                            QK_SCALE * LOG2_E)
        ckv_t = jnp.swapaxes(ckv.astype(BF16).reshape(bp, tp // tq, tq, MLA_KV_LORA), 2, 3)
        yb_p = _attn_prompt_call(q_cat, k_cat, ckv_t, wp["w_uvpt"], l, tq)
        yac_p, conv_st, gla_st = _seq_call(conv_o, gla_o, conv0, gla0, wp, l, ln_p)
        outs["ckv_p"].append(ckv)
        outs["kr_p"].append(kr128[:, :, MLA_NOPE:MLA_NOPE + MLA_ROPE])
        outs["conv_p"].append(conv_st)
        outs["gla_p"].append(gla_st)
        x1_p, h_p, idx_p, gate_p = _out_call(yac_p, yb_p, xp, mod_p, wp, l, 1, tt_p)

        conv_o, mla_o, gla_o = _in_call(xs, mod_s, wp["g_mix"], wp["w_in_p"], l, bb_s, ts)
        q_cat, _, ckv, kr128 = _qk_call(mla_o, cos_s, sin_s, lambda i, j: 0, wp, l, bb_s, ts, F32, QK_SCALE)
        kr_s = kr128[:, :, MLA_NOPE:MLA_NOPE + MLA_ROPE]
        yb_s = _attn_sample_call(page_table, q_cat, ckv, kr_s, wp, cache_ckv, cache_krt, l, chunk_s)
        yac_s, conv_st, gla_st = _seq_call(conv_o, gla_o, state_conv[l], state_gla[l], wp, l, ts)
        outs["ckv_s"].append(ckv)
        outs["kr_s"].append(kr_s)
        outs["conv_s"].append(conv_st)
        outs["gla_s"].append(gla_st)
        x1_s, h_s, idx_s, gate_s = _out_call(yac_s, yb_s, xs, mod_s, wp, l, bb_s, ts)

        h_all = jnp.concatenate([h_p, h_s], axis=0)
        top_i = jnp.concatenate([idx_p[:, 0:TOP_K], idx_s[:, 0:TOP_K]], axis=0)
        gates = jnp.concatenate([gate_p, gate_s], axis=0)
        rt, p_rows = _route(top_i, n_all)
        h_tiles = h_all.reshape(n_all, TOKEN_TILE, LANES)
        y_sorted = _experts_call(rt, h_tiles, w_gate, b_gate, w_up, b_up, w_down, b_down, l, p_rows)
        y_sorted = y_sorted.reshape(p_rows, TOKEN_TILE, LANES)
        xp = _combine_call(rt["pos"], gates, x1_p, mod_p, y_sorted, 0, 1, rows_c)
        xs = _combine_call(rt["pos"], gates, x1_s, mod_s, y_sorted, n_p, rows_c // ts, ts)

    st = lambda k: jnp.stack(outs[k])
    return (xp, xs, st("ckv_p"), st("kr_p"), st("conv_p"), st("gla_p"),
            st("ckv_s"), st("kr_s"), st("conv_s"), st("gla_s"))
```

```python
import functools

import numpy as np
import jax
import jax.numpy as jnp
from jax import lax
from jax.experimental import pallas as pl
from jax.experimental.pallas import tpu as pltpu

F32 = jnp.float32
BF16 = jnp.bfloat16

D_MODEL = 1024
CONV_DIM = 256
CONV_W = 3
MLA_HEADS = 8
MLA_NOPE = 64
MLA_ROPE = 32
MLA_V = 64
MLA_Q_LORA = 256
MLA_KV_LORA = 128
GLA_HEADS = 4
GLA_DK = 32
GLA_DV = 64
GLA_GATE_RANK = 16
GLA_NORMALIZER = 16.0
N_EXPERTS = 32
TOP_K = 4
D_FF = 1024
SWIGLU_ALPHA = 1.702
SWIGLU_LIMIT = 7.0
ROPE_THETA = 10000.0
EPS = 1e-6
N_MOD = 6
PAGE_SIZE = 128
IN_SIZES = (256, 256, 256, 256, 128, 32, 128, 128, 256, 16, 256)

LANES = 128
SUBLANES = 8

W_CONV_OUT = 3 * CONV_DIM
W_MLA_OUT = MLA_Q_LORA + MLA_KV_LORA + LANES
W_GLA_OUT = 128 + 128 + 256 + LANES + 256
W_IN_OUT = W_CONV_OUT + W_MLA_OUT + W_GLA_OUT
HEAD_W = LANES
QK_SCALE = (MLA_NOPE + MLA_ROPE) ** -0.5
LOG2_E = 1.4426950408889634
NEG_BIG = -1e30

MOE_TM = 256
COMBINE_TOK = 128
TOKEN_TILE = D_MODEL // LANES
DMA_UNROLL = 8
ONES_ROWS = 16
SAMPLE_CHUNK_UNROLL = 8
ATTN_HEAD_GROUPS = 1


def _dot(a, b):
    return jnp.dot(a, b, preferred_element_type=F32)


def _dot_nt(a, b):
    return lax.dot_general(a, b, (((1,), (1,)), ((), ())), preferred_element_type=F32)


def _dot_tn(a, b):
    return lax.dot_general(a, b, (((0,), (0,)), ((), ())), preferred_element_type=F32)


def _sigmoid(x):
    return 1.0 / (1.0 + jnp.exp(-x))


def _params(sem, vmem_mb):
    return pltpu.CompilerParams(dimension_semantics=sem, vmem_limit_bytes=vmem_mb * 1024 * 1024)


def _mod_kernel(c_ref, w_ref, b_ref, o_ref):
    c = c_ref[...]
    a = (c * _sigmoid(c)).astype(BF16)
    o_ref[...] = _dot(a, w_ref[...].astype(BF16)) + b_ref[...]


def _mod_call(c_all, w_mod, b_mod):
    depth = w_mod.shape[0]
    nb = c_all.shape[0]
    return pl.pallas_call(
        _mod_kernel,
        grid=(depth, N_MOD),
        in_specs=[
            pl.BlockSpec((nb, D_MODEL), lambda l, j: (0, 0)),
            pl.BlockSpec((None, D_MODEL, D_MODEL), lambda l, j: (l, 0, j)),
            pl.BlockSpec((None, 1, D_MODEL), lambda l, j: (l, 0, j)),
        ],
        out_specs=pl.BlockSpec((None, nb, D_MODEL), lambda l, j: (l, 0, j)),
        out_shape=jax.ShapeDtypeStruct((depth, nb, N_MOD * D_MODEL), F32),
        compiler_params=_params(("arbitrary", "arbitrary"), 40),
        name="adaln_mod",
    )(c_all, w_mod, b_mod.reshape(depth, 1, N_MOD * D_MODEL))


def _in_kernel(x_ref, mod_ref, g_ref, w_ref, oc_ref, om_ref, og_ref):
    bb, tt, _ = x_ref.shape
    x = x_ref[...]
    ms = jnp.mean(x * x, axis=-1, keepdims=True)
    y = x * lax.rsqrt(ms + EPS) * g_ref[...]
    h = y * (1.0 + mod_ref[:, 1:2, :]) + mod_ref[:, 0:1, :]
    h2 = h.reshape(bb * tt, D_MODEL).astype(BF16)
    o = _dot(h2, w_ref[...])
    oc_ref[...] = o[:, 0:W_CONV_OUT].reshape(bb, tt, W_CONV_OUT)
    om_ref[...] = o[:, W_CONV_OUT:W_CONV_OUT + W_MLA_OUT].reshape(bb, tt, W_MLA_OUT)
    og_ref[...] = o[:, W_CONV_OUT + W_MLA_OUT:].reshape(bb, tt, W_GLA_OUT)


def _in_call(x, mod, g_mix, w_in_p, layer, bb, tt):
    b, t, _ = x.shape
    bs3 = lambda w: pl.BlockSpec((bb, tt, w), lambda i, j: (i, j, 0))
    return pl.pallas_call(
        _in_kernel,
        grid=(b // bb, t // tt),
        in_specs=[
            bs3(D_MODEL),
            pl.BlockSpec((bb, N_MOD, D_MODEL), lambda i, j: (i, 0, 0)),
            pl.BlockSpec((None, 1, D_MODEL), lambda i, j: (layer, 0, 0)),
            pl.BlockSpec((None, D_MODEL, W_IN_OUT), lambda i, j: (layer, 0, 0)),
        ],
        out_specs=[bs3(W_CONV_OUT), bs3(W_MLA_OUT), bs3(W_GLA_OUT)],
        out_shape=[jax.ShapeDtypeStruct((b, t, W_CONV_OUT), F32),
                   jax.ShapeDtypeStruct((b, t, W_MLA_OUT), F32),
                   jax.ShapeDtypeStruct((b, t, W_GLA_OUT), F32)],
        compiler_params=_params(("arbitrary", "arbitrary"), 48),
        name="in_proj",
    )(x, mod, g_mix, w_in_p)


def _segsum(x2, p_ref):
    return _dot(x2.astype(BF16), p_ref[...])


def _swap_rope_halves(x):
    lane = lax.broadcasted_iota(jnp.int32, x.shape, 1)
    return jnp.where(lane < 80, pltpu.roll(x, LANES - 16, 1), pltpu.roll(x, 16, 1))


def _qk_kernel(mla_ref, cos_ref, sin_ref, gqa_ref, wuq_ref, gq_ref, gkva_ref, wuk_ref, gk_ref, gkr_ref,
               p_ref, invn_ref, q_out, k_out, ckv_out, kr_out, *, q_scale):
    bb, tt, _ = mla_ref.shape
    rows = bb * tt
    mla = mla_ref[...].reshape(rows, W_MLA_OUT)
    cos = cos_ref[...]
    sin = sin_ref[...]
    invn = invn_ref[...]

    def head_norm_rope(xh, gain):
        rstd = lax.rsqrt(_segsum(xh * xh, p_ref) * invn + EPS)
        xn = xh * rstd * gain
        return xn * cos + _swap_rope_halves(xn) * sin

    q_lat = mla[:, 0:MLA_Q_LORA]
    qa = q_lat * lax.rsqrt(jnp.mean(q_lat * q_lat, axis=-1, keepdims=True) + EPS) * gqa_ref[...]
    q = _dot(qa.astype(BF16), wuq_ref[...])
    kv_lat = mla[:, MLA_Q_LORA:MLA_Q_LORA + MLA_KV_LORA]
    ckv = kv_lat * lax.rsqrt(jnp.mean(kv_lat * kv_lat, axis=-1, keepdims=True) + EPS) * gkva_ref[...]
    ckv_out[...] = ckv.reshape(bb, tt, MLA_KV_LORA)
    kr = head_norm_rope(mla[:, MLA_Q_LORA + MLA_KV_LORA:], gkr_ref[...])
    kr_out[...] = kr.reshape(bb, tt, LANES)
    k = _dot(ckv.astype(BF16), wuk_ref[...])
    for h in range(MLA_HEADS):
        qh = head_norm_rope(q[:, h * HEAD_W:(h + 1) * HEAD_W], gq_ref[...]) * q_scale
        q_out[:, h, :, :] = qh.astype(q_out.dtype).reshape(bb, tt, HEAD_W)
        kh = k[:, h * HEAD_W:(h + 1) * HEAD_W]
        rstd = lax.rsqrt(_segsum(kh * kh, p_ref) * invn + EPS)
        k_out[:, h, :, :] = (kh * rstd * gk_ref[...] + kr).astype(k_out.dtype).reshape(bb, tt, HEAD_W)


def _qk_call(mla, cos_t, sin_t, tab_index, wp, layer, bb, tt, qk_dtype, q_scale):
    b, t, _ = mla.shape
    rows = bb * tt
    vec = lambda arr: pl.BlockSpec((None, 1, arr.shape[-1]), lambda i, j: (layer, 0, 0))
    mat = lambda arr: pl.BlockSpec((None,) + arr.shape[1:], lambda i, j: (layer, 0, 0))
    tab = pl.BlockSpec((rows, LANES), lambda i, j: (tab_index(i, j), 0))
    const2 = lambda arr: pl.BlockSpec(arr.shape, lambda i, j: (0, 0))
    hk = pl.BlockSpec((bb, MLA_HEADS, tt, HEAD_W), lambda i, j: (i, 0, j, 0))
    return pl.pallas_call(
        functools.partial(_qk_kernel, q_scale=q_scale),
        grid=(b // bb, t // tt),
        in_specs=[pl.BlockSpec((bb, tt, W_MLA_OUT), lambda i, j: (i, j, 0)), tab, tab,
                  vec(wp["g_qa"]), mat(wp["w_uq"]), vec(wp["gq_lane"]), vec(wp["g_kva"]), mat(wp["w_uk"]),
                  vec(wp["gk_lane"]), vec(wp["gkr_lane"]), const2(wp["p_seg"]), const2(wp["invn"])],
        out_specs=[hk, hk,
                   pl.BlockSpec((bb, tt, MLA_KV_LORA), lambda i, j: (i, j, 0)),
                   pl.BlockSpec((bb, tt, LANES), lambda i, j: (i, j, 0))],
        out_shape=[jax.ShapeDtypeStruct((b, MLA_HEADS, t, HEAD_W), qk_dtype),
                   jax.ShapeDtypeStruct((b, MLA_HEADS, t, HEAD_W), qk_dtype),
                   jax.ShapeDtypeStruct((b, t, MLA_KV_LORA), F32),
                   jax.ShapeDtypeStruct((b, t, LANES), F32)],
        compiler_params=_params(("arbitrary", "arbitrary"), 40),
        name="mla_prep",
    )(mla, cos_t, sin_t, wp["g_qa"], wp["w_uq"], wp["gq_lane"], wp["g_kva"], wp["w_uk"], wp["gk_lane"],
      wp["gkr_lane"], wp["p_seg"], wp["invn"])


def _attn_prompt_kernel(q_ref, k_ref, vt_ref, wuvt_ref, o_ref, m_s, acc_s, *, tq):
    qi = pl.program_id(1)
    hq = MLA_HEADS * tq
    m_s[...] = jnp.full(m_s.shape, -jnp.inf, F32)
    acc_s[...] = jnp.zeros(acc_s.shape, F32)

    hg = MLA_HEADS // ATTN_HEAD_GROUPS
    gw = hg * tq

    def block(kb, masked):
        off = pl.multiple_of(kb * tq, tq)
        vt = vt_ref[kb]
        for g in range(ATTN_HEAD_GROUPS):
            lanes = slice(g * gw, (g + 1) * gw)
            st = jnp.concatenate(
                [_dot_nt(k_ref[h, pl.ds(off, tq), :], q_ref[h]) for h in range(g * hg, (g + 1) * hg)],
                axis=1)
            if masked:
                key = lax.broadcasted_iota(jnp.int32, (tq, gw), 0)
                qry = lax.rem(lax.broadcasted_iota(jnp.int32, (tq, gw), 1), tq)
                st = jnp.where(key <= qry, st, -jnp.inf)
            m_old = m_s[:, lanes]
            m_new = jnp.maximum(m_old, jnp.max(st, axis=0, keepdims=True))
            p = jnp.exp2(st - m_new)
            alpha = jnp.exp2(m_old - m_new)
            m_s[:, lanes] = m_new
            acc_s[:, lanes] = alpha * acc_s[:, lanes] + _dot(vt, p.astype(BF16))

    def body(kb, _):
        block(kb, False)
        return 0

    lax.fori_loop(0, qi, body, 0)
    block(qi, True)
    ot = (acc_s[0:MLA_KV_LORA, :] / acc_s[MLA_KV_LORA:MLA_KV_LORA + 1, :]).astype(BF16)
    yt = jnp.zeros((MLA_HEADS * MLA_V, tq), F32)
    for h in range(MLA_HEADS):
        yt = yt + _dot(wuvt_ref[h], ot[:, h * tq:(h + 1) * tq])
    o_ref[...] = jnp.transpose(yt)


def _attn_prompt_call(q_cat, k_cat, ckv_t, w_uvpt, layer, tq):
    b, _, t, _ = q_cat.shape
    nkb = t // tq
    return pl.pallas_call(
        functools.partial(_attn_prompt_kernel, tq=tq),
        grid=(b, nkb),
        in_specs=[
            pl.BlockSpec((None, MLA_HEADS, tq, HEAD_W), lambda i, j: (i, 0, j, 0)),
            pl.BlockSpec((None, MLA_HEADS, t, HEAD_W), lambda i, j: (i, 0, 0, 0)),
            pl.BlockSpec((None, nkb, MLA_KV_LORA + ONES_ROWS, tq), lambda i, j: (i, 0, 0, 0)),
            pl.BlockSpec((None, MLA_HEADS, MLA_HEADS * MLA_V, MLA_KV_LORA), lambda i, j: (layer, 0, 0, 0)),
        ],
        out_specs=pl.BlockSpec((None, tq, MLA_HEADS * MLA_V), lambda i, j: (i, j, 0)),
        out_shape=jax.ShapeDtypeStruct((b, t, MLA_HEADS * MLA_V), F32),
        scratch_shapes=[pltpu.VMEM((1, MLA_HEADS * tq), F32),
                        pltpu.VMEM((MLA_KV_LORA + ONES_ROWS, MLA_HEADS * tq), F32)],
        compiler_params=_params(("arbitrary", "arbitrary"), 48),
        name="attn_prompt",
    )(q_cat, k_cat, ckv_t, w_uvpt)


def _attn_sample_kernel(pt_ref, q_ref, ckvn_ref, krn_ref, wabs_ref, wukt_ref, wuv_ref,
                        cache_ckv, cache_krt, o_ref, ckv_buf, krt_buf, s_buf, sem,
                        *, layer, n_pages, chunk, ts):
    b = pl.program_id(0)
    nb = pl.num_programs(0)
    past = n_pages * PAGE_SIZE
    hq = MLA_HEADS * ts
    ppc = chunk // PAGE_SIZE
    n_chunks = past // chunk

    def page_copies(seq, slot, j):
        page = pt_ref[seq, j]
        ckv_dst = ckv_buf.at[slot, pl.ds(j * PAGE_SIZE, PAGE_SIZE), :]
        krt_dst = krt_buf.at[slot, j // ppc, :, pl.ds((j % ppc) * PAGE_SIZE, PAGE_SIZE)]
        return (pltpu.make_async_copy(cache_ckv.at[layer, page], ckv_dst, sem.at[slot, 0]),
                pltpu.make_async_copy(cache_krt.at[layer, page], krt_dst, sem.at[slot, 1]))

    def start_seq(seq, slot):
        for j in range(n_pages):
            c0, c1 = page_copies(seq, slot, j)
            c0.start()
            c1.start()

    def wait_seq(seq, slot):
        for j in range(n_pages):
            c0, c1 = page_copies(seq, slot, j)
            c0.wait()
            c1.wait()

    slot = lax.rem(b, 2)

    @pl.when(b == 0)
    def _():
        start_seq(0, 0)

    @pl.when(b + 1 < nb)
    def _():
        start_seq(b + 1, 1 - slot)

    wait_seq(b, slot)

    q_all = q_ref[...].reshape(hq, HEAD_W)
    q_wide = _dot(q_all.astype(BF16), wabs_ref[...])
    q_abs = jnp.concatenate(
        [q_wide[h * ts:(h + 1) * ts, h * MLA_KV_LORA:(h + 1) * MLA_KV_LORA] for h in range(MLA_HEADS)],
        axis=0).astype(BF16)
    q_rope = q_all[:, MLA_NOPE:MLA_NOPE + MLA_ROPE].astype(BF16)
    w_stack = jnp.concatenate([wukt_ref[...], q_abs], axis=0)
    nk = MLA_HEADS * MLA_NOPE

    def scores(c, rope_scores):
        res = _dot_nt(w_stack, c)
        k2 = res[0:nk, :] * res[0:nk, :]
        r_rows = []
        for h in range(MLA_HEADS):
            ssq = jnp.sum(k2[h * MLA_NOPE:(h + 1) * MLA_NOPE, :], axis=0, keepdims=True)
            r = lax.rsqrt(ssq * (1.0 / MLA_NOPE) + EPS)
            r_rows.append(jnp.broadcast_to(r, (ts, r.shape[1])))
        return res[nk:, :] * jnp.concatenate(r_rows, axis=0) + rope_scores

    def fold_lanes(x, op):
        out = x[:, 0:LANES]
        for i in range(1, x.shape[1] // LANES):
            out = op(out, x[:, i * LANES:(i + 1) * LANES])
        return out

    unroll = next(u for u in (SAMPLE_CHUNK_UNROLL, 2, 1) if n_chunks % u == 0)

    def pass1(it, m_part):
        for u in range(unroll):
            ci = it * unroll + u
            off = pl.multiple_of(ci * chunk, chunk)
            c = ckv_buf[slot, pl.ds(off, chunk), :].astype(BF16)
            s = scores(c, _dot(q_rope, krt_buf[slot, ci].astype(BF16)))
            s_buf[ci] = s
            m_part = jnp.maximum(m_part, fold_lanes(s, jnp.maximum))
        return m_part

    m_part = lax.fori_loop(0, n_chunks // unroll, pass1, jnp.full((hq, LANES), -jnp.inf, F32))
    c_new = jnp.concatenate([ckvn_ref[...], jnp.zeros((LANES - ts, MLA_KV_LORA), F32)], axis=0).astype(BF16)
    kr_new = jnp.concatenate([krn_ref[...], jnp.zeros((LANES - ts, MLA_ROPE), F32)], axis=0).astype(BF16)
    s_new = scores(c_new, _dot_nt(q_rope, kr_new))
    jcol = lax.broadcasted_iota(jnp.int32, (hq, LANES), 1)
    trow = lax.rem(lax.broadcasted_iota(jnp.int32, (hq, LANES), 0), ts)
    s_new = jnp.where(jcol <= trow, s_new, -jnp.inf)
    m = jnp.maximum(jnp.max(m_part, axis=1, keepdims=True), jnp.max(s_new, axis=1, keepdims=True))

    def pass2(it, carry):
        acc, l_part = carry
        for u in range(unroll):
            ci = it * unroll + u
            off = pl.multiple_of(ci * chunk, chunk)
            p = jnp.exp(s_buf[ci] - m)
            acc = acc + _dot(p.astype(BF16), ckv_buf[slot, pl.ds(off, chunk), :].astype(BF16))
            l_part = l_part + fold_lanes(p, jnp.add)
        return acc, l_part

    acc, l_part = lax.fori_loop(0, n_chunks // unroll, pass2,
                                (jnp.zeros((hq, MLA_KV_LORA), F32), jnp.zeros((hq, LANES), F32)))
    p_new = jnp.exp(s_new - m)
    acc = acc + _dot(p_new.astype(BF16), c_new)
    l = jnp.sum(l_part, axis=1, keepdims=True) + jnp.sum(p_new, axis=1, keepdims=True)
    o = (acc / l).astype(BF16)
    y = jnp.zeros((ts, MLA_HEADS * MLA_V), F32)
    for h in range(MLA_HEADS):
        y = y + _dot(o, wuv_ref[h])[h * ts:(h + 1) * ts, :]
    o_ref[...] = y


def _attn_sample_call(page_table, q_cat, ckv_new, kr_new, wp, cache_ckv, cache_krt, layer, chunk):
    b, _, ts, _ = q_cat.shape
    n_pages = page_table.shape[1]
    past = n_pages * PAGE_SIZE
    hq = MLA_HEADS * ts
    kern = functools.partial(_attn_sample_kernel, layer=layer, n_pages=n_pages, chunk=chunk, ts=ts)
    lay3 = lambda arr: pl.BlockSpec((None,) + arr.shape[1:], lambda i, pt: (layer,) + (0,) * (arr.ndim - 1))
    grid_spec = pltpu.PrefetchScalarGridSpec(
        num_scalar_prefetch=1,
        grid=(b,),
        in_specs=[
            pl.BlockSpec((None, MLA_HEADS, ts, HEAD_W), lambda i, pt: (i, 0, 0, 0)),
            pl.BlockSpec((None, ts, MLA_KV_LORA), lambda i, pt: (i, 0, 0)),
            pl.BlockSpec((None, ts, MLA_ROPE), lambda i, pt: (i, 0, 0)),
            lay3(wp["w_abs"]), lay3(wp["w_uk_t"]),
            lay3(wp["w_uvp"]),
            pl.BlockSpec(memory_space=pl.ANY),
            pl.BlockSpec(memory_space=pl.ANY),
        ],
        out_specs=pl.BlockSpec((None, ts, MLA_HEADS * MLA_V), lambda i, pt: (i, 0, 0)),
        scratch_shapes=[
            pltpu.VMEM((2, past, MLA_KV_LORA), F32),
            pltpu.VMEM((2, past // chunk, MLA_ROPE, chunk), F32),
            pltpu.VMEM((past // chunk, hq, chunk), F32),
            pltpu.SemaphoreType.DMA((2, 2)),
        ],
    )
    return pl.pallas_call(
        kern,
        grid_spec=grid_spec,
        out_shape=jax.ShapeDtypeStruct((b, ts, MLA_HEADS * MLA_V), F32),
        compiler_params=_params(("arbitrary",), 56),
        name="attn_sample",
    )(page_table, q_cat, ckv_new, kr_new, wp["w_abs"], wp["w_uk_t"], wp["w_uvp"], cache_ckv, cache_krt)


def _cumsum_rows(x):
    n = x.shape[0]
    row = lax.broadcasted_iota(jnp.int32, x.shape, 0)
    d = 1
    while d < n:
        x = x + jnp.where(row >= d, pltpu.roll(x, d, 0), 0.0)
        d *= 2
    return x


def _seq_kernel(conv_ref, gla_ref, cprev_ref, s0_ref, wconv_ref, wa2_ref, ba_ref, go_ref,
                segexp_ref, bdmask_ref, p256_ref,
                y_ref, cstate_ref, sstate_ref, s_scr, c_scr, v_scr):
    ci = pl.program_id(1)
    nci = pl.num_programs(1)
    ln = conv_ref.shape[0]
    kd = GLA_HEADS * GLA_DK
    vd = GLA_HEADS * GLA_DV

    @pl.when(ci == 0)
    def _():
        c_scr[...] = cprev_ref[...]
        s_scr[...] = jnp.zeros(s_scr.shape, F32)
        for h in range(GLA_HEADS):
            s_scr[h * GLA_DK:(h + 1) * GLA_DK, h * GLA_DV:(h + 1) * GLA_DV] = s0_ref[h]

    conv = conv_ref[...]
    v = conv[:, CONV_DIM:2 * CONV_DIM] * conv[:, 0:CONV_DIM]
    row = lax.broadcasted_iota(jnp.int32, (ln, CONV_DIM), 0)
    prev = c_scr[...]
    p0 = jnp.broadcast_to(prev[0:1, :], (ln, CONV_DIM))
    p1 = jnp.broadcast_to(prev[1:2, :], (ln, CONV_DIM))
    v1 = jnp.where(row == 0, p1, pltpu.roll(v, 1, 0))
    v2 = jnp.where(row == 0, p0, jnp.where(row == 1, p1, pltpu.roll(v, 2, 0)))
    wc = wconv_ref[...]
    y_a = conv[:, 2 * CONV_DIM:] * (v2 * wc[0:1, :] + v1 * wc[1:2, :] + v * wc[2:3, :])
    v_scr[...] = v
    c_scr[...] = v_scr[ln - 2:ln, :]
    y_ref[:, 0:CONV_DIM] = y_a

    gla = gla_ref[...]
    gq = gla[:, 0:kd] * (GLA_DK ** -0.5)
    gk = gla[:, kd:2 * kd]
    gv = gla[:, 2 * kd:2 * kd + vd]
    ga = gla[:, 2 * kd + vd:2 * kd + vd + LANES]
    gr = gla[:, 2 * kd + vd + LANES:]
    x = _dot(ga.astype(BF16), wa2_ref[...]) + ba_ref[...]
    log_a = (jnp.minimum(x, 0.0) - jnp.log(1.0 + jnp.exp(-jnp.abs(x)))) * (1.0 / GLA_NORMALIZER)
    cum = _cumsum_rows(log_a)
    s_prev = s_scr[...]
    o = _dot((gq * jnp.exp(cum)).astype(BF16), s_prev.astype(BF16))
    segexp = segexp_ref[...]
    rowk = lax.broadcasted_iota(jnp.int32, (ln, kd), 0)
    o_intra = [jnp.zeros((ln - g0, vd), F32) for g0 in range(0, ln, SUBLANES)]
    for s in range(ln):
        g = s // SUBLANES
        g0 = g * SUBLANES
        dec = jnp.exp(jnp.minimum(cum[g0:, :] - cum[s:s + 1, :], 0.0))
        dmat = jnp.where(rowk[g0:, :] >= s, gq[g0:, :] * gk[s:s + 1, :] * dec, 0.0)
        att = _dot(dmat.astype(BF16), segexp)
        o_intra[g] = o_intra[g] + att * gv[s:s + 1, :]
    intra = jnp.zeros((ln, vd), F32)
    for g, part in enumerate(o_intra):
        g0 = g * SUBLANES
        if g0 == 0:
            intra = intra + part
        else:
            intra = intra + jnp.concatenate([jnp.zeros((g0, vd), F32), part], axis=0)
    o = o + intra
    last = cum[ln - 1:ln, :]
    kdec = gk * jnp.exp(last - cum)
    e_col = jnp.transpose(jnp.broadcast_to(jnp.exp(last), (kd, kd)))
    e_col = jnp.concatenate([e_col, e_col], axis=1)
    s_new = e_col * s_prev + bdmask_ref[...] * _dot_tn(kdec.astype(BF16), gv.astype(BF16))
    s_scr[...] = s_new
    rstd = lax.rsqrt(_dot((o * o).astype(BF16), p256_ref[...]) * (1.0 / GLA_DV) + EPS)
    y_ref[:, CONV_DIM:] = o * rstd * go_ref[...] * (gr * _sigmoid(gr))

    @pl.when(ci == nci - 1)
    def _():
        cstate_ref[...] = c_scr[...]
        for h in range(GLA_HEADS):
            sstate_ref[h] = s_scr[h * GLA_DK:(h + 1) * GLA_DK, h * GLA_DV:(h + 1) * GLA_DV]


def _seq_call(conv, gla, conv_prev, s0, wp, layer, ln):
    b, t, _ = conv.shape
    kd = GLA_HEADS * GLA_DK
    vd = GLA_HEADS * GLA_DV
    lay = lambda arr: pl.BlockSpec((None,) + arr.shape[1:], lambda i, j: (layer,) + (0,) * (arr.ndim - 1))
    const2 = lambda arr: pl.BlockSpec(arr.shape, lambda i, j: (0, 0))
    return pl.pallas_call(
        _seq_kernel,
        grid=(b, t // ln),
        in_specs=[
            pl.BlockSpec((None, ln, W_CONV_OUT), lambda i, j: (i, j, 0)),
            pl.BlockSpec((None, ln, W_GLA_OUT), lambda i, j: (i, j, 0)),
            pl.BlockSpec((None, CONV_W - 1, CONV_DIM), lambda i, j: (i, 0, 0)),
            pl.BlockSpec((None, GLA_HEADS, GLA_DK, GLA_DV), lambda i, j: (i, 0, 0, 0)),
            lay(wp["w_conv"]), lay(wp["w_a2p"]), lay(wp["b_a"]), lay(wp["g_gla_lane"]),
            const2(wp["seg_exp"]), const2(wp["bd_mask"]), const2(wp["p256"]),
        ],
        out_specs=[
            pl.BlockSpec((None, ln, CONV_DIM + vd), lambda i, j: (i, j, 0)),
            pl.BlockSpec((None, CONV_W - 1, CONV_DIM), lambda i, j: (i, 0, 0)),
            pl.BlockSpec((None, GLA_HEADS, GLA_DK, GLA_DV), lambda i, j: (i, 0, 0, 0)),
        ],
        out_shape=[jax.ShapeDtypeStruct((b, t, CONV_DIM + vd), F32),
                   jax.ShapeDtypeStruct((b, CONV_W - 1, CONV_DIM), F32),
                   jax.ShapeDtypeStruct((b, GLA_HEADS, GLA_DK, GLA_DV), F32)],
        scratch_shapes=[pltpu.VMEM((kd, vd), F32), pltpu.VMEM((CONV_W - 1, CONV_DIM), F32),
                        pltpu.VMEM((ln, CONV_DIM), F32)],
        compiler_params=_params(("arbitrary", "arbitrary"), 40),
        name="conv_gla",
    )(conv, gla, conv_prev, s0, wp["w_conv"], wp["w_a2p"], wp["b_a"], wp["g_gla_lane"],
      wp["seg_exp"], wp["bd_mask"], wp["p256"])


def _store_token_tiles(ref, x):
    rows = x.shape[0]
    for s in range(TOKEN_TILE):
        ref[pl.ds(s, rows, stride=TOKEN_TILE), :] = x[:, s * LANES:(s + 1) * LANES]


def _load_token_tiles(ref, base, rows):
    return jnp.concatenate(
        [ref[pl.ds(base + s, rows, stride=TOKEN_TILE), :] for s in range(TOKEN_TILE)], axis=1)


def _split_bf16(x):
    hi = x.astype(BF16)
    lo = (x - hi.astype(F32)).astype(BF16)
    return hi, lo


def _out_kernel(yac_ref, yb_ref, x_ref, mod_ref, woac_ref, wob_ref, g_ref, wrh_ref, wrl_ref, br_ref,
                x1_ref, h_ref, idx_ref, gate_ref):
    bb, tt, _ = x_ref.shape
    rows = bb * tt
    yac = yac_ref[...].reshape(rows, 2 * CONV_DIM).astype(BF16)
    yb = yb_ref[...].reshape(rows, MLA_HEADS * MLA_V).astype(BF16)
    y = (_dot(yac, woac_ref[...]) + _dot(yb, wob_ref[...])).reshape(bb, tt, D_MODEL)
    x1 = x_ref[...] + mod_ref[:, 2:3, :] * y
    x1_ref[...] = x1
    ms = jnp.mean(x1 * x1, axis=-1, keepdims=True)
    h = x1 * lax.rsqrt(ms + EPS) * g_ref[...]
    h = (h * (1.0 + mod_ref[:, 4:5, :]) + mod_ref[:, 3:4, :]).reshape(rows, D_MODEL)
    _store_token_tiles(h_ref, h)
    h_hi, h_lo = _split_bf16(h)
    logits = _dot(h_hi, wrh_ref[...]) + (_dot(h_hi, wrl_ref[...]) + _dot(h_lo, wrh_ref[...])) + br_ref[...]
    lane = lax.broadcasted_iota(jnp.int32, (rows, LANES), 1)
    lane_f = lane.astype(F32)
    idx_out = jnp.zeros((rows, LANES), F32)
    val_out = jnp.zeros((rows, LANES), F32)
    vals = []
    for k in range(TOP_K):
        mx = jnp.max(logits, axis=-1, keepdims=True)
        ix = jnp.min(jnp.where(logits == mx, lane_f, float(LANES)), axis=-1, keepdims=True)
        idx_out = jnp.where(lane == k, ix, idx_out)
        vals.append(mx)
        logits = jnp.where(lane_f == ix, -jnp.inf, logits)
    es = [jnp.exp(v - vals[0]) for v in vals]
    den = es[0] + es[1] + es[2] + es[3]
    for k in range(TOP_K):
        val_out = jnp.where(lane == k, es[k] / den, val_out)
    idx_ref[...] = idx_out.astype(jnp.int32)
    gate_ref[...] = val_out


def _out_call(yac, yb, x, mod, wp, layer, bb, tt):
    b, t, _ = x.shape
    n = b * t
    rows = bb * tt
    nj = t // tt
    bs3 = lambda w: pl.BlockSpec((bb, tt, w), lambda i, j: (i, j, 0))
    lay = lambda arr: pl.BlockSpec((None,) + arr.shape[1:], lambda i, j: (layer,) + (0,) * (arr.ndim - 1))
    flat = lambda w: pl.BlockSpec((rows, w), lambda i, j: (i * nj + j, 0))
    return pl.pallas_call(
        _out_kernel,
        grid=(b // bb, nj),
        in_specs=[bs3(2 * CONV_DIM), bs3(MLA_HEADS * MLA_V), bs3(D_MODEL),
                  pl.BlockSpec((bb, N_MOD, D_MODEL), lambda i, j: (i, 0, 0)),
                  lay(wp["w_o_ac"]), lay(wp["w_o_b"]), lay(wp["g_ffn"]),
                  lay(wp["w_r_hi"]), lay(wp["w_r_lo"]), lay(wp["b_r"])],
        out_specs=[bs3(D_MODEL), pl.BlockSpec((rows * TOKEN_TILE, LANES), lambda i, j: (i * nj + j, 0)),
                   flat(LANES), flat(LANES)],
        out_shape=[jax.ShapeDtypeStruct((b, t, D_MODEL), F32),
                   jax.ShapeDtypeStruct((n * TOKEN_TILE, LANES), F32),
                   jax.ShapeDtypeStruct((n, LANES), jnp.int32),
                   jax.ShapeDtypeStruct((n, LANES), F32)],
        compiler_params=_params(("arbitrary", "arbitrary"), 48),
        name="out_proj_router",
    )(yac, yb, x, mod, wp["w_o_ac"], wp["w_o_b"], wp["g_ffn"], wp["w_r_hi"], wp["w_r_lo"], wp["b_r"])


def _row_dma_loop(n_rows, make_copy, start):
    def body(g, _):
        for u in range(DMA_UNROLL):
            cp = make_copy(g * DMA_UNROLL + u)
            if start:
                cp.start()
            else:
                cp.wait()
        return 0
    lax.fori_loop(0, n_rows // DMA_UNROLL, body, 0)


def _experts_kernel(stok_ref, be_ref, src0_ref, nrows_ref, nv_ref, h_hbm, h2d_hbm, wg_ref, bg_ref, wu_ref, bu_ref,
                    wd_ref, bd_ref, o_ref, xbuf, wg_s, wu_s, wd_s, sem):
    i = pl.program_id(0)
    nv = nv_ref[0]

    def start_gather(blk, slot):
        first = src0_ref[blk]
        last = first + nrows_ref[blk] - 1

        def make_copy(r):
            tok = stok_ref[jnp.minimum(first + r, last)]
            dst = xbuf.at[slot, pl.ds(pl.multiple_of(r * TOKEN_TILE, TOKEN_TILE), TOKEN_TILE), :]
            return pltpu.make_async_copy(h_hbm.at[tok], dst, sem.at[slot])

        _row_dma_loop(MOE_TM, make_copy, True)

    def wait_gather(slot):
        pltpu.make_async_copy(h2d_hbm.at[pl.ds(0, MOE_TM * TOKEN_TILE)], xbuf.at[slot], sem.at[slot]).wait()

    slot = lax.rem(i, 2)

    @pl.when(i == 0)
    def _():
        start_gather(0, 0)

    @pl.when(i + 1 < nv)
    def _():
        start_gather(i + 1, 1 - slot)

    @pl.when(i >= nv)
    def _():
        o_ref[...] = jnp.zeros(o_ref.shape, F32)

    @pl.when(i < nv)
    def _():
        wait_gather(slot)
        prev = be_ref[jnp.maximum(i - 1, 0)]

        @pl.when((i == 0) | (be_ref[i] != prev))
        def _():
            wg_s[...] = wg_ref[...].astype(BF16)
            wu_s[...] = wu_ref[...].astype(BF16)
            wd_s[...] = wd_ref[...].astype(BF16)

        x = _load_token_tiles(xbuf.at[slot], 0, MOE_TM).astype(BF16)
        g = _dot(x, wg_s[...]) + bg_ref[...]
        u = _dot(x, wu_s[...]) + bu_ref[...]
        g = jnp.minimum(g, SWIGLU_LIMIT)
        u = jnp.clip(u, -SWIGLU_LIMIT, SWIGLU_LIMIT)
        a = (u + 1.0) * (g * _sigmoid(SWIGLU_ALPHA * g))
        _store_token_tiles(o_ref, _dot(a.astype(BF16), wd_s[...]) + bd_ref[...])


def _experts_call(rt, h_tiles, w_gate, b_gate, w_up, b_up, w_down, b_down, layer, p_rows):
    nblk = p_rows // MOE_TM
    eff = lambda i, nv: jnp.minimum(i, nv[0] - 1)
    wspec = pl.BlockSpec((None, None, D_MODEL, D_FF),
                         lambda i, st, be, s0, nr, nv: (layer, be[eff(i, nv)], 0, 0))
    bspec = pl.BlockSpec((None, None, 1, D_FF), lambda i, st, be, s0, nr, nv: (layer, be[eff(i, nv)], 0, 0))
    grid_spec = pltpu.PrefetchScalarGridSpec(
        num_scalar_prefetch=5,
        grid=(nblk,),
        in_specs=[pl.BlockSpec(memory_space=pl.ANY), pl.BlockSpec(memory_space=pl.ANY),
                  wspec, bspec, wspec, bspec, wspec, bspec],
        out_specs=pl.BlockSpec((MOE_TM * TOKEN_TILE, LANES), lambda i, st, be, s0, nr, nv: (i, 0)),
        scratch_shapes=[pltpu.VMEM((2, MOE_TM * TOKEN_TILE, LANES), F32),
                        pltpu.VMEM((D_MODEL, D_FF), BF16), pltpu.VMEM((D_MODEL, D_FF), BF16),
                        pltpu.VMEM((D_FF, D_MODEL), BF16), pltpu.SemaphoreType.DMA((2,))],
    )
    depth = w_gate.shape[0]
    b4 = lambda bias: bias.reshape(depth, N_EXPERTS, 1, bias.shape[-1])
    return pl.pallas_call(
        _experts_kernel,
        grid_spec=grid_spec,
        out_shape=jax.ShapeDtypeStruct((p_rows * TOKEN_TILE, LANES), F32),
        compiler_params=_params(("arbitrary",), 56),
        name="moe_experts",
    )(rt["stok"], rt["blk_e"], rt["src0"], rt["nrows"], rt["n_valid"],
      h_tiles.reshape(-1, TOKEN_TILE, LANES), h_tiles.reshape(-1, LANES),
      w_gate, b4(b_gate), w_up, b4(b_up), w_down, b4(b_down))


def _combine_kernel(pos_ref, posn_ref, gate_ref, x1_ref, mod_ref, yb_hbm, yb2d_hbm, o_ref, buf, sem):
    i = pl.program_id(0)
    bb, tt, _ = x1_ref.shape
    rows = bb * tt
    slot = lax.rem(i, 2)

    def start_gather(p_ref, slot):
        def make_copy(r):
            dst = buf.at[slot, pl.ds(pl.multiple_of(r * TOKEN_TILE, TOKEN_TILE), TOKEN_TILE), :]
            return pltpu.make_async_copy(yb_hbm.at[p_ref[0, 0, r]], dst, sem.at[slot])
        _row_dma_loop(rows * TOP_K, make_copy, True)

    @pl.when(i == 0)
    def _():
        start_gather(pos_ref, 0)

    @pl.when(i + 1 < pl.num_programs(0))
    def _():
        start_gather(posn_ref, 1 - slot)

    pltpu.make_async_copy(yb2d_hbm.at[pl.ds(0, rows * TOP_K * TOKEN_TILE)], buf.at[slot], sem.at[slot]).wait()
    gates = gate_ref[...]
    tile = buf.at[slot]
    y = _load_token_tiles(tile, 0, rows) * gates[:, 0:1]
    for k in range(1, TOP_K):
        y = y + _load_token_tiles(tile, k * rows * TOKEN_TILE, rows) * gates[:, k:k + 1]
    o_ref[...] = x1_ref[...] + mod_ref[:, 5:6, :] * y.reshape(bb, tt, D_MODEL)


def _combine_call(pos, gates, x1, mod, yb, tok0, bb, tt):
    b, t, _ = x1.shape
    rows = bb * tt
    nj = t // tt
    n_steps = (b // bb) * nj
    blk0 = tok0 // rows
    n_all = pos.shape[0] // TOP_K
    pos3 = jnp.swapaxes(pos.reshape(n_all // rows, rows, TOP_K), 1, 2).reshape(n_all // rows, 1, rows * TOP_K)
    pos_spec = lambda nxt: pl.BlockSpec(
        (1, 1, rows * TOP_K), lambda i: (blk0 + jnp.minimum(i + nxt, n_steps - 1), 0, 0), memory_space=pltpu.SMEM)
    return pl.pallas_call(
        _combine_kernel,
        grid=(n_steps,),
        in_specs=[
            pos_spec(0), pos_spec(1),
            pl.BlockSpec((rows, LANES), lambda i: (blk0 + i, 0)),
            pl.BlockSpec((bb, tt, D_MODEL), lambda i: (i // nj, i % nj, 0)),
            pl.BlockSpec((bb, N_MOD, D_MODEL), lambda i: (i // nj, 0, 0)),
            pl.BlockSpec(memory_space=pl.ANY),
            pl.BlockSpec(memory_space=pl.ANY),
        ],
        out_specs=pl.BlockSpec((bb, tt, D_MODEL), lambda i: (i // nj, i % nj, 0)),
        out_shape=jax.ShapeDtypeStruct((b, t, D_MODEL), F32),
        scratch_shapes=[pltpu.VMEM((2, TOP_K * rows * TOKEN_TILE, LANES), F32), pltpu.SemaphoreType.DMA((2,))],
        compiler_params=_params(("arbitrary",), 32),
        name="moe_combine",
    )(pos3, pos3, gates, x1, mod, yb.reshape(-1, TOKEN_TILE, LANES), yb.reshape(-1, LANES))


def _route(top_i, n_tokens):
    m = n_tokens * TOP_K
    flat_e = top_i.reshape(m)
    onehot = (flat_e[:, None] == jnp.arange(N_EXPERTS, dtype=jnp.int32)[None, :]).astype(jnp.int32)
    csum = jnp.cumsum(onehot, axis=0)
    rank = jnp.sum(onehot * csum, axis=1) - 1
    counts = csum[-1]
    padded = (counts + MOE_TM - 1) // MOE_TM * MOE_TM
    pends = jnp.cumsum(padded)
    pstarts = pends - padded
    pos = jnp.sum(onehot * pstarts[None, :], axis=1) + rank
    p_rows = (m // MOE_TM + N_EXPERTS) * MOE_TM
    nblk = p_rows // MOE_TM
    blk_start = jnp.arange(nblk, dtype=jnp.int32) * MOE_TM
    blk_e = jnp.minimum(jnp.sum((blk_start[:, None] >= pends[None, :]).astype(jnp.int32), axis=1), N_EXPERTS - 1)
    n_valid = (pends[-1] // MOE_TM).reshape(1)
    slot = jnp.arange(m, dtype=jnp.int32)
    stok = (jnp.sort(flat_e * m + slot) % m) // TOP_K
    starts = jnp.cumsum(counts) - counts
    off = blk_start - pstarts[blk_e]
    src0 = jnp.clip(starts[blk_e] + off, 0, m - 1)
    nrows = jnp.clip(counts[blk_e] - off, 1, MOE_TM)
    route = dict(pos=pos, blk_e=blk_e, n_valid=n_valid, stok=stok, src0=src0, nrows=nrows)
    return {k: v.astype(jnp.int32) for k, v in route.items()}, p_rows


def _prep_weights(g_mix, g_ffn, w_in, w_o, w_conv, g_qa, w_uq, g_qn, g_qr, g_kva, g_kr, w_uk, g_kn, w_uv,
                  w_a2, b_a, g_gla_o, w_router, b_router):
    depth = w_in.shape[0]
    offs = np.cumsum((0,) + IN_SIZES)
    seg = lambda i: w_in[:, :, offs[i]:offs[i + 1]]
    z = lambda n: jnp.zeros((depth, D_MODEL, n), w_in.dtype)
    w_in_p = jnp.concatenate(
        [seg(0), seg(1), seg(2), seg(3), seg(4), z(64), seg(5), z(32),
         seg(6), seg(7), seg(8), seg(9), z(LANES - GLA_GATE_RANK), seg(10)], axis=-1).astype(BF16)
    pad_last = lambda a, n: jnp.pad(a, [(0, 0)] * (a.ndim - 1) + [(0, n - a.shape[-1])])
    v3 = lambda a: a.reshape(depth, 1, a.shape[-1])
    wp = {}
    wp["g_mix"] = v3(g_mix)
    wp["g_ffn"] = v3(g_ffn)
    wp["w_in_p"] = w_in_p
    wp["g_qa"] = v3(g_qa)
    wp["w_uq"] = pad_last(w_uq, HEAD_W).reshape(depth, MLA_Q_LORA, MLA_HEADS * HEAD_W).astype(BF16)
    wp["gq_lane"] = v3(pad_last(jnp.concatenate([g_qn, g_qr], axis=-1), HEAD_W))
    wp["g_kva"] = v3(g_kva)
    wp["w_uk"] = pad_last(w_uk, HEAD_W).reshape(depth, MLA_KV_LORA, MLA_HEADS * HEAD_W).astype(BF16)
    wp["w_uk_t"] = jnp.swapaxes(w_uk.reshape(depth, MLA_KV_LORA, MLA_HEADS * MLA_NOPE), 1, 2).astype(BF16)
    wp["gk_lane"] = v3(pad_last(g_kn, HEAD_W))
    wp["gkr_lane"] = v3(pad_last(jnp.concatenate([jnp.zeros((depth, MLA_NOPE), F32), g_kr], axis=-1), HEAD_W))
    w_abs = jnp.transpose(w_uk, (0, 3, 2, 1)) * g_kn[:, :, None, None]
    w_abs = w_abs.reshape(depth, MLA_NOPE, MLA_HEADS * MLA_KV_LORA)
    wp["w_abs"] = jnp.pad(w_abs, ((0, 0), (0, HEAD_W - MLA_NOPE), (0, 0))).astype(BF16)
    eye_h = jnp.eye(MLA_HEADS, dtype=F32)
    wp["w_uvp"] = jnp.einsum("lrhv,hg->lhrgv", w_uv, eye_h).reshape(
        depth, MLA_HEADS, MLA_KV_LORA, MLA_HEADS * MLA_V).astype(BF16)
    wp["w_uvpt"] = jnp.swapaxes(wp["w_uvp"], 2, 3)
    wp["w_o_ac"] = jnp.concatenate([w_o[:, 0:CONV_DIM], w_o[:, CONV_DIM + MLA_HEADS * MLA_V:]], axis=1).astype(BF16)
    wp["w_o_b"] = w_o[:, CONV_DIM:CONV_DIM + MLA_HEADS * MLA_V].astype(BF16)
    wp["w_conv"] = w_conv
    wp["w_a2p"] = jnp.pad(w_a2, ((0, 0), (0, LANES - GLA_GATE_RANK), (0, 0))).astype(BF16)
    wp["b_a"] = v3(b_a)
    wp["g_gla_lane"] = v3(jnp.tile(g_gla_o, (1, GLA_HEADS)))
    w_r = pad_last(w_router, LANES)
    w_r_hi = w_r.astype(BF16)
    wp["w_r_hi"] = w_r_hi
    wp["w_r_lo"] = (w_r - w_r_hi.astype(F32)).astype(BF16)
    wp["b_r"] = v3(jnp.concatenate([b_router, jnp.full((depth, LANES - N_EXPERTS), NEG_BIG, F32)], axis=-1))
    lane = np.arange(LANES)
    seg_id = np.where(lane < MLA_NOPE, 0, np.where(lane < MLA_NOPE + MLA_ROPE, 1, 2 + lane))
    wp["p_seg"] = jnp.asarray((seg_id[:, None] == seg_id[None, :]).astype(np.float32), BF16)
    wp["invn"] = jnp.asarray(np.where(lane < MLA_NOPE, 1.0 / MLA_NOPE, 1.0 / MLA_ROPE).astype(np.float32))[None, :]
    kd = GLA_HEADS * GLA_DK
    vd = GLA_HEADS * GLA_DV
    hk = np.arange(kd) // GLA_DK
    hv = np.arange(vd) // GLA_DV
    bd = (hk[:, None] == hv[None, :]).astype(np.float32)
    wp["seg_exp"] = jnp.asarray(bd, BF16)
    wp["bd_mask"] = jnp.asarray(bd, F32)
    wp["p256"] = jnp.asarray((hv[:, None] == hv[None, :]).astype(np.float32), BF16)
    return wp


def _rope_tables(pos):
    half = MLA_ROPE // 2
    freqs = ROPE_THETA ** (-jnp.arange(half, dtype=F32) / half)
    ang = pos.astype(F32)[:, None] * freqs
    cos, sin = jnp.cos(ang), jnp.sin(ang)
    t = pos.shape[0]
    cos_t = jnp.concatenate([jnp.ones((t, MLA_NOPE), F32), cos, cos, jnp.zeros((t, HEAD_W - 96), F32)], axis=-1)
    sin_t = jnp.concatenate([jnp.zeros((t, MLA_NOPE), F32), -sin, sin, jnp.zeros((t, HEAD_W - 96), F32)], axis=-1)
    return cos_t, sin_t


def kernel(x_prompt, x_sample, cache_ckv, cache_kr, state_conv, state_gla, page_table, c_prompt, c_sample, g_mix, g_ffn, w_mod, b_mod, w_in, w_o, w_conv, g_qa, w_uq, g_qn, g_qr, g_kva, g_kr, w_uk, g_kn, w_uv, w_a2, b_a, g_gla_o, w_router, b_router, w_gate, b_gate, w_up, b_up, w_down, b_down):
    depth = w_in.shape[0]
    bp, tp, _ = x_prompt.shape
    bs, ts, _ = x_sample.shape
    n_pages = page_table.shape[1]
    past = n_pages * PAGE_SIZE
    n_p, n_s = bp * tp, bs * ts
    n_all = n_p + n_s

    tt_p = min(512, tp)
    bb_s = min(16, bs)
    tq = min(512, tp)
    ln_p = min(64, tp)
    chunk_s = min(1024, max(PAGE_SIZE, past // 2))
    rows_c = COMBINE_TOK

    wp = _prep_weights(g_mix, g_ffn, w_in, w_o, w_conv, g_qa, w_uq, g_qn, g_qr, g_kva, g_kr, w_uk, g_kn, w_uv,
                       w_a2, b_a, g_gla_o, w_router, b_router)
    cache_krt = jnp.swapaxes(cache_kr, 2, 3)

    nb_mod = -(-(bp + bs) // 16) * 16
    c_all = jnp.concatenate([c_prompt, c_sample, jnp.zeros((nb_mod - bp - bs, D_MODEL), F32)], axis=0)
    mod_all = _mod_call(c_all, w_mod, b_mod)

    cos_p, sin_p = _rope_tables(jnp.arange(tp, dtype=jnp.int32))
    cos_s, sin_s = _rope_tables(past + jnp.arange(ts, dtype=jnp.int32))
    cos_s, sin_s = jnp.tile(cos_s, (bb_s, 1)), jnp.tile(sin_s, (bb_s, 1))

    conv0 = jnp.zeros((bp, CONV_W - 1, CONV_DIM), F32)
    gla0 = jnp.zeros((bp, GLA_HEADS, GLA_DK, GLA_DV), F32)

    xp, xs = x_prompt, x_sample
    outs = {k: [] for k in ("ckv_p", "kr_p", "conv_p", "gla_p", "ckv_s", "kr_s", "conv_s", "gla_s")}
    for l in range(depth):
        mod_p = mod_all[l, 0:bp].reshape(bp, N_MOD, D_MODEL)
        mod_s = mod_all[l, bp:bp + bs].reshape(bs, N_MOD, D_MODEL)

        conv_o, mla_o, gla_o = _in_call(xp, mod_p, wp["g_mix"], wp["w_in_p"], l, 1, tt_p)
        q_cat, k_cat, ckv, kr128 = _qk_call(mla_o, cos_p, sin_p, lambda i, j: j, wp, l, 1, tt_p, BF16,
                                            QK_SCALE * LOG2_E)
        ckv_t = jnp.swapaxes(ckv.astype(BF16).reshape(bp, tp // tq, tq, MLA_KV_LORA), 2, 3)
        ckv_t = jnp.concatenate([ckv_t, jnp.ones((bp, tp // tq, ONES_ROWS, tq), BF16)], axis=2)
        yb_p = _attn_prompt_call(q_cat, k_cat, ckv_t, wp["w_uvpt"], l, tq)
        yac_p, conv_st, gla_st = _seq_call(conv_o, gla_o, conv0, gla0, wp, l, ln_p)
        outs["ckv_p"].append(ckv)
        outs["kr_p"].append(kr128[:, :, MLA_NOPE:MLA_NOPE + MLA_ROPE])
        outs["conv_p"].append(conv_st)
        outs["gla_p"].append(gla_st)
        x1_p, h_p, idx_p, gate_p = _out_call(yac_p, yb_p, xp, mod_p, wp, l, 1, tt_p)

        conv_o, mla_o, gla_o = _in_call(xs, mod_s, wp["g_mix"], wp["w_in_p"], l, bb_s, ts)
        q_cat, _, ckv, kr128 = _qk_call(mla_o, cos_s, sin_s, lambda i, j: 0, wp, l, bb_s, ts, F32, QK_SCALE)
        kr_s = kr128[:, :, MLA_NOPE:MLA_NOPE + MLA_ROPE]
        yb_s = _attn_sample_call(page_table, q_cat, ckv, kr_s, wp, cache_ckv, cache_krt, l, chunk_s)
        yac_s, conv_st, gla_st = _seq_call(conv_o, gla_o, state_conv[l], state_gla[l], wp, l, ts)
        outs["ckv_s"].append(ckv)
        outs["kr_s"].append(kr_s)
        outs["conv_s"].append(conv_st)
        outs["gla_s"].append(gla_st)
        x1_s, h_s, idx_s, gate_s = _out_call(yac_s, yb_s, xs, mod_s, wp, l, bb_s, ts)

        h_all = jnp.concatenate([h_p, h_s], axis=0)
        top_i = jnp.concatenate([idx_p[:, 0:TOP_K], idx_s[:, 0:TOP_K]], axis=0)
        gates = jnp.concatenate([gate_p, gate_s], axis=0)
        rt, p_rows = _route(top_i, n_all)
        h_tiles = h_all.reshape(n_all, TOKEN_TILE, LANES)
        y_sorted = _experts_call(rt, h_tiles, w_gate, b_gate, w_up, b_up, w_down, b_down, l, p_rows)
        y_sorted = y_sorted.reshape(p_rows, TOKEN_TILE, LANES)
        xp = _combine_call(rt["pos"], gates, x1_p, mod_p, y_sorted, 0, 1, rows_c)
        xs = _combine_call(rt["pos"], gates, x1_s, mod_s, y_sorted, n_p, rows_c // ts, ts)

    st = lambda k: jnp.stack(outs[k])
    return (xp, xs, st("ckv_p"), st("kr_p"), st("conv_p"), st("gla_p"),
            st("ckv_s"), st("kr_s"), st("conv_s"), st("gla_s"))
```
